```python
import jax, jax.numpy as jnp
from jax import lax
import numpy as np


D_MODEL = 2048
BATCH = 2
SEQ = 4096
DEPTH = 4

N_META = 16
GLA_HEADS = 4
GLA_DK = D_MODEL // 2
GLA_DV = D_MODEL
GLA_DK_HEAD = GLA_DK // GLA_HEADS
GLA_DV_HEAD = GLA_DV // GLA_HEADS
GLA_LOWRANK = 16
GLA_TAU = 16.0
GLA_CHUNK = 64
CONV_CH = D_MODEL
CONV_WIDTH = 31
D_FF = 5632
N_EXPERTS = 8
TOP_K = 2
MOE_BLOCK = 256
N_DENSE = (DEPTH + 1) // 2
N_MOE = DEPTH // 2
DEEPNORM_ALPHA = (2.0 * DEPTH) ** 0.25
DEEPNORM_BETA = (8.0 * DEPTH) ** -0.25
LN_EPS = 1e-5
IN_WIDTHS = (GLA_DK, GLA_DK, GLA_DV, GLA_DV, GLA_LOWRANK, 2 * CONV_CH, D_MODEL, D_MODEL)
IN_SPLITS = tuple(sum(IN_WIDTHS[:i + 1]) for i in range(len(IN_WIDTHS) - 1))
D_IN = sum(IN_WIDTHS)

kernel_name = 'hybrid_gla_conformer_moe_deepnorm'


def layer_norm(x, g, b):
    xf = x.astype(jnp.float32)
    mu = xf.mean(-1, keepdims=True)
    var = jnp.square(xf - mu).mean(-1, keepdims=True)
    return ((xf - mu) * lax.rsqrt(var + LN_EPS) * g.astype(jnp.float32) + b.astype(jnp.float32)).astype(x.dtype)


def rms_norm(x, g):
    xf = x.astype(jnp.float32)
    return xf * lax.rsqrt(jnp.square(xf).mean(-1, keepdims=True) + LN_EPS) * g.astype(jnp.float32)


def gla_branch(q, k, v, r, a_low, w_alpha_up, b_alpha, norm_g, w_o):
    Bt, L, _ = q.shape
    log_a = jax.nn.log_sigmoid((a_low @ w_alpha_up + b_alpha).astype(jnp.float32)) / GLA_TAU
    pad = (-N_META) % GLA_CHUNK
    n_chunks = (L + pad) // GLA_CHUNK

    def to_chunks(t):
        t = jnp.pad(t.astype(jnp.float32), ((0, 0), (pad, 0), (0, 0)))
        t = t.reshape(Bt, n_chunks, GLA_CHUNK, GLA_HEADS, -1)
        return t.transpose(0, 3, 1, 2, 4)

    qc = to_chunks(q) * (GLA_DK_HEAD ** -0.5)
    kc = to_chunks(k)
    vc = to_chunks(v)
    b = jnp.cumsum(to_chunks(log_a), axis=3)
    b_last = b[:, :, :, -1:, :]
    q_dec = qc * jnp.exp(b)
    k_dec = kc * jnp.exp(-b)
    k_to_end = kc * jnp.exp(b_last - b)
    causal = jnp.tril(jnp.ones((GLA_CHUNK, GLA_CHUNK), dtype=bool))
    scores = jnp.where(causal, jnp.einsum('bhnid,bhnjd->bhnij', q_dec, k_dec), 0.0)
    o_intra = jnp.einsum('bhnij,bhnjv->bhniv', scores, vc)

    def chunk_step(S, inp):
        qd, kd, vch, decay = inp
        o = jnp.einsum('bhid,bhdv->bhiv', qd, S)
        S_new = decay[..., None] * S + jnp.einsum('bhjd,bhjv->bhdv', kd, vch)
        return S_new, o

    xs = (jnp.moveaxis(q_dec, 2, 0), jnp.moveaxis(k_to_end, 2, 0), jnp.moveaxis(vc, 2, 0),
          jnp.moveaxis(jnp.exp(b_last[:, :, :, 0, :]), 2, 0))
    S0 = jnp.zeros((Bt, GLA_HEADS, GLA_DK_HEAD, GLA_DV_HEAD), jnp.float32)
    _, o_inter = lax.scan(chunk_step, S0, xs)
    o = o_intra + jnp.moveaxis(o_inter, 0, 2)
    o = o.transpose(0, 2, 3, 1, 4).reshape(Bt, n_chunks * GLA_CHUNK, GLA_HEADS, GLA_DV_HEAD)[:, pad:]
    o = rms_norm(o, norm_g.reshape(GLA_HEADS, GLA_DV_HEAD)).reshape(Bt, L, GLA_DV).astype(v.dtype)
    return (jax.nn.silu(r) * o) @ w_o


def conformer_conv_branch(u_glu, conv_w, conv_b, norm_g, norm_b, w_o):
    a, g = jnp.split(u_glu, 2, axis=-1)
    u = a * jax.nn.sigmoid(g)
    u = jnp.pad(u, ((0, 0), (CONV_WIDTH - 1, 0), (0, 0)))
    y = lax.conv_general_dilated(u, conv_w[:, None, :], window_strides=(1,), padding='VALID',
                                 dimension_numbers=('NWC', 'WIO', 'NWC'),
                                 feature_group_count=CONV_CH) + conv_b
    y = jax.nn.silu(layer_norm(y, norm_g, norm_b))
    return y @ w_o


def hybrid_mixer(h, w_in, w_alpha_up, b_alpha, gla_norm_g, w_gla_o, conv_w, conv_b,
                 conv_norm_g, conv_norm_b, w_conv_o, w_out):
    p = h @ w_in
    q, k, v, r, a_low, glu, ga, gb = jnp.split(p, IN_SPLITS, axis=-1)
    y_a = gla_branch(q, k, v, r, a_low, w_alpha_up, b_alpha, gla_norm_g, w_gla_o)
    y_b = conformer_conv_branch(glu, conv_w, conv_b, conv_norm_g, conv_norm_b, w_conv_o)
    m = jax.nn.sigmoid(ga) * y_a + jax.nn.sigmoid(gb) * y_b
    return m @ w_out


def swiglu(h, w1, w3, w2):
    return (jax.nn.silu(h @ w1) * (h @ w3)) @ w2


def moe_swiglu(h, router_w, router_b, w1, w3, w2):
    Bt, L, D = h.shape
    xt = h.reshape(Bt * L, D)
    n_tok = Bt * L
    n_assign = n_tok * TOP_K
    logits = xt.astype(jnp.float32) @ router_w.astype(jnp.float32) + router_b.astype(jnp.float32)
    top_logit, top_exp = lax.top_k(logits, TOP_K)
    top_gate = jax.nn.softmax(top_logit, axis=-1)
    exp_flat = top_exp.reshape(n_assign)
    tok_flat = jnp.repeat(jnp.arange(n_tok, dtype=jnp.int32), TOP_K)
    gate_flat = top_gate.reshape(n_assign)
    order = jnp.argsort(exp_flat)
    exp_sorted = exp_flat[order]
    counts = jnp.bincount(exp_flat, length=N_EXPERTS)
    padded = (counts + MOE_BLOCK - 1) // MOE_BLOCK * MOE_BLOCK
    start = jnp.cumsum(counts) - counts
    pad_end = jnp.cumsum(padded)
    pad_start = pad_end - padded
    dest = pad_start[exp_sorted] + (jnp.arange(n_assign) - start[exp_sorted])
    n_blocks = -(-n_assign // MOE_BLOCK) + N_EXPERTS
    n_slots = n_blocks * MOE_BLOCK
    slot_tok = jnp.zeros((n_slots,), jnp.int32).at[dest].set(tok_flat[order])
    slot_gate = jnp.zeros((n_slots,), jnp.float32).at[dest].set(gate_flat[order])
    block_exp = jnp.minimum(jnp.searchsorted(pad_end, jnp.arange(n_blocks) * MOE_BLOCK, side='right'),
                            N_EXPERTS - 1)

    def expert_block(args):
        tok, gate, e = args
        xb = xt[tok]
        hb = jax.nn.silu(xb @ w1[e]) * (xb @ w3[e])
        return (hb @ w2[e]) * gate[:, None].astype(xb.dtype)

    y = lax.map(expert_block, (slot_tok.reshape(n_blocks, MOE_BLOCK),
                               slot_gate.reshape(n_blocks, MOE_BLOCK), block_exp))
    out = jnp.zeros_like(xt).at[slot_tok].add(y.reshape(n_slots, D))
    return out.reshape(Bt, L, D)


def setup_inputs(seed: int = 0) -> dict:
    key = jax.random.key(seed)
    ks = jax.random.split(key, 32)
    d = D_MODEL

    def nrm(k, shape, scale):
        return jax.random.normal(k, shape, jnp.float32) * scale

    def gain(k, shape):
        return 1.0 + nrm(k, shape, 0.02)

    return {
        'x': nrm(ks[0], (BATCH, SEQ, d), 1.0),
        'meta_tokens': nrm(ks[1], (N_META, d), 1.0),
        'ln_in_g': gain(ks[2], (d,)),
        'ln_in_b': nrm(ks[3], (d,), 0.02),
        'w_in': nrm(ks[4], (DEPTH, d, D_IN), d ** -0.5),
        'w_alpha_up': nrm(ks[5], (DEPTH, GLA_LOWRANK, GLA_DK), GLA_LOWRANK ** -0.5),
        'b_alpha': nrm(ks[6], (DEPTH, GLA_DK), 0.1),
        'gla_norm_g': gain(ks[7], (DEPTH, GLA_DV)),
        'w_gla_o': nrm(ks[8], (DEPTH, GLA_DV, d), GLA_DV ** -0.5),
        'conv_w': nrm(ks[9], (DEPTH, CONV_WIDTH, CONV_CH), CONV_WIDTH ** -0.5),
        'conv_b': nrm(ks[10], (DEPTH, CONV_CH), 0.02),
        'conv_norm_g': gain(ks[11], (DEPTH, CONV_CH)),
        'conv_norm_b': nrm(ks[12], (DEPTH, CONV_CH), 0.02),
        'w_conv_o': nrm(ks[13], (DEPTH, CONV_CH, d), CONV_CH ** -0.5),
        'w_out': nrm(ks[14], (DEPTH, d, d), DEEPNORM_BETA * d ** -0.5),
        'ln_mix_g': gain(ks[15], (DEPTH, d)),
        'ln_mix_b': nrm(ks[16], (DEPTH, d), 0.02),
        'ffn_w1': nrm(ks[17], (N_DENSE, d, D_FF), d ** -0.5),
        'ffn_w3': nrm(ks[18], (N_DENSE, d, D_FF), d ** -0.5),
        'ffn_w2': nrm(ks[19], (N_DENSE, D_FF, d), DEEPNORM_BETA * D_FF ** -0.5),
        'router_w': nrm(ks[20], (N_MOE, d, N_EXPERTS), d ** -0.5),
        'router_b': nrm(ks[21], (N_MOE, N_EXPERTS), 0.01),
        'moe_w1': nrm(ks[22], (N_MOE, N_EXPERTS, d, D_FF), d ** -0.5),
        'moe_w3': nrm(ks[23], (N_MOE, N_EXPERTS, d, D_FF), d ** -0.5),
        'moe_w2': nrm(ks[24], (N_MOE, N_EXPERTS, D_FF, d), DEEPNORM_BETA * D_FF ** -0.5),
        'ln_ffn_g': gain(ks[25], (DEPTH, d)),
        'ln_ffn_b': nrm(ks[26], (DEPTH, d), 0.02),
    }


def reference(x, meta_tokens, ln_in_g, ln_in_b, w_in, w_alpha_up, b_alpha, gla_norm_g, w_gla_o,
              conv_w, conv_b, conv_norm_g, conv_norm_b, w_conv_o, w_out, ln_mix_g, ln_mix_b,
              ffn_w1, ffn_w3, ffn_w2, router_w, router_b, moe_w1, moe_w3, moe_w2,
              ln_ffn_g, ln_ffn_b):
    Bt = x.shape[0]
    meta = jnp.broadcast_to(meta_tokens[None].astype(x.dtype), (Bt, N_META, D_MODEL))
    h = jnp.concatenate([meta, x], axis=1)
    h = layer_norm(h, ln_in_g, ln_in_b)
    for i in range(DEPTH):
        mix = hybrid_mixer(h, w_in[i], w_alpha_up[i], b_alpha[i], gla_norm_g[i], w_gla_o[i],
                           conv_w[i], conv_b[i], conv_norm_g[i], conv_norm_b[i], w_conv_o[i], w_out[i])
        h = layer_norm(DEEPNORM_ALPHA * h + mix, ln_mix_g[i], ln_mix_b[i])
        j = i // 2
        if i % 2 == 0:
            f = swiglu(h, ffn_w1[j], ffn_w3[j], ffn_w2[j])
        else:
            f = moe_swiglu(h, router_w[j], router_b[j], moe_w1[j], moe_w3[j], moe_w2[j])
        h = layer_norm(DEEPNORM_ALPHA * h + f, ln_ffn_g[i], ln_ffn_b[i])
    return h[:, N_META:]
```

```python
import functools

import jax
import jax.numpy as jnp
from jax import lax
from jax.experimental import pallas as pl
from jax.experimental.pallas import tpu as pltpu

F32 = jnp.float32
BF16 = jnp.bfloat16
I32 = jnp.int32

D = 2048
BATCH = 2
SEQ = 4096
DEPTH = 4
N_META = 16
HEADS = 4
DK = 1024
DV = 2048
DKH = DK // HEADS
DVH = DV // HEADS
LOWRANK = 16
TAU = 16.0
CHUNK = 64
CONV_W = 31
DFF = 5632
NE = 8
ALPHA = (2.0 * DEPTH) ** 0.25
EPS = 1e-5
D_IN = 14352
COL_ALOW = 6144
COL_REST = 6160

R_MAIN = BATCH * SEQ
META0 = R_MAIN
N_TOK = R_MAIN + N_META
RB = 256
R = 8448
NRB = R // RB
META_BLK = META0 // RB
BLK_PER_SEQ = SEQ // RB
BIG = 768

MOE_BLK = 256
N_ASSIGN = 2 * N_TOK
MOE_NBLK = -(-N_ASSIGN // MOE_BLK) + NE
N_SLOTS = MOE_NBLK * MOE_BLK

VMEM_LIMIT = 56 * 1024 * 1024


def _cparams(sem):
    return pltpu.CompilerParams(dimension_semantics=sem, vmem_limit_bytes=VMEM_LIMIT)


def _sigmoid(x):
    return 1.0 / (1.0 + jnp.exp(-x))


def _layer_norm(z, g, b):
    mu = jnp.mean(z, axis=-1, keepdims=True)
    zc = z - mu
    var = jnp.mean(zc * zc, axis=-1, keepdims=True)
    return zc * lax.rsqrt(var + EPS) * g + b


def _ln_in_kernel(x_ref, g_ref, b_ref, h_ref, hb_ref):
    h = _layer_norm(x_ref[...], g_ref[...], b_ref[...])
    h_ref[...] = h
    hb_ref[...] = h.astype(BF16)


def ln_in(x, g, b):
    return pl.pallas_call(
        _ln_in_kernel,
        grid=(NRB,),
        in_specs=[pl.BlockSpec((RB, D), lambda i: (i, 0)),
                  pl.BlockSpec((1, D), lambda i: (0, 0)),
                  pl.BlockSpec((1, D), lambda i: (0, 0))],
        out_specs=[pl.BlockSpec((RB, D), lambda i: (i, 0)),
                   pl.BlockSpec((RB, D), lambda i: (i, 0))],
        out_shape=[jax.ShapeDtypeStruct((R, D), F32), jax.ShapeDtypeStruct((R, D), BF16)],
        compiler_params=_cparams(("arbitrary",)),
        name="ln_in",
    )(x, g.reshape(1, D), b.reshape(1, D))


def _res_ln_kernel(h_ref, f_ref, g_ref, b_ref, ho_ref, hb_ref):
    z = ALPHA * h_ref[...] + f_ref[...]
    h = _layer_norm(z, g_ref[...], b_ref[...])
    ho_ref[...] = h
    hb_ref[...] = h.astype(BF16)


def res_ln(h, f, g, b):
    return pl.pallas_call(
        _res_ln_kernel,
        grid=(NRB,),
        in_specs=[pl.BlockSpec((RB, D), lambda i: (i, 0)),
                  pl.BlockSpec((RB, D), lambda i: (i, 0)),
                  pl.BlockSpec((1, D), lambda i: (0, 0)),
                  pl.BlockSpec((1, D), lambda i: (0, 0))],
        out_specs=[pl.BlockSpec((RB, D), lambda i: (i, 0)),
                   pl.BlockSpec((RB, D), lambda i: (i, 0))],
        out_shape=[jax.ShapeDtypeStruct((R, D), F32), jax.ShapeDtypeStruct((R, D), BF16)],
        compiler_params=_cparams(("arbitrary",)),
        name="res_ln",
    )(h, f, g.reshape(1, D), b.reshape(1, D))


def _weight_changed(be_ref, b):
    prev = be_ref[jnp.maximum(b - 1, 0)]
    return jnp.logical_or(b == 0, be_ref[b] != prev)


def _gmm_kernel(be_ref, nu_ref, x_ref, w_ref, o_ref, wb_ref):
    b = pl.program_id(1)

    @pl.when(_weight_changed(be_ref, b))
    def _():
        wb_ref[...] = w_ref[...].astype(BF16)

    @pl.when(b < nu_ref[0])
    def _():
        x = x_ref[...].astype(BF16)
        o_ref[...] = jnp.dot(x, wb_ref[...], preferred_element_type=F32).astype(o_ref.dtype)

    @pl.when(b >= nu_ref[0])
    def _():
        o_ref[...] = jnp.zeros_like(o_ref)


def gmm(x, w, be, nu, *, blk, tn, n_out, col_off=0, out_dtype=BF16, name="gmm"):
    rows, k = x.shape
    grid = (n_out // tn, rows // blk)
    return pl.pallas_call(
        _gmm_kernel,
        grid_spec=pltpu.PrefetchScalarGridSpec(
            num_scalar_prefetch=2,
            grid=grid,
            in_specs=[pl.BlockSpec((blk, k), lambda j, b, be, nu: (jnp.minimum(b, nu[0] - 1), 0)),
                      pl.BlockSpec((None, k, tn), lambda j, b, be, nu: (be[b], 0, j + col_off))],
            out_specs=pl.BlockSpec((blk, tn), lambda j, b, be, nu: (b, j)),
            scratch_shapes=[pltpu.VMEM((k, tn), BF16)]),
        out_shape=jax.ShapeDtypeStruct((rows, n_out), out_dtype),
        compiler_params=_cparams(("arbitrary", "arbitrary")),
        name=name,
    )(be, nu, x, w)


def _gateup_kernel(be_ref, nu_ref, x_ref, w1_ref, w3_ref, o_ref, w1b_ref, w3b_ref):
    b = pl.program_id(1)

    @pl.when(_weight_changed(be_ref, b))
    def _():
        w1b_ref[...] = w1_ref[...].astype(BF16)
        w3b_ref[...] = w3_ref[...].astype(BF16)

    @pl.when(b < nu_ref[0])
    def _():
        x = x_ref[...].astype(BF16)
        a = jnp.dot(x, w1b_ref[...], preferred_element_type=F32)
        c = jnp.dot(x, w3b_ref[...], preferred_element_type=F32)
        o_ref[...] = ((a * _sigmoid(a)) * c).astype(o_ref.dtype)

    @pl.when(b >= nu_ref[0])
    def _():
        o_ref[...] = jnp.zeros_like(o_ref)


def gateup(x, w1, w3, be, nu, *, blk, tn, name="gateup"):
    rows, k = x.shape
    n_out = w1.shape[-1]
    grid = (n_out // tn, rows // blk)
    wspec = pl.BlockSpec((None, k, tn), lambda j, b, be, nu: (be[b], 0, j))
    return pl.pallas_call(
        _gateup_kernel,
        grid_spec=pltpu.PrefetchScalarGridSpec(
            num_scalar_prefetch=2,
            grid=grid,
            in_specs=[pl.BlockSpec((blk, k), lambda j, b, be, nu: (jnp.minimum(b, nu[0] - 1), 0)),
                      wspec, wspec],
            out_specs=pl.BlockSpec((blk, tn), lambda j, b, be, nu: (b, j)),
            scratch_shapes=[pltpu.VMEM((k, tn), BF16), pltpu.VMEM((k, tn), BF16)]),
        out_shape=jax.ShapeDtypeStruct((rows, n_out), BF16),
        compiler_params=_cparams(("arbitrary", "arbitrary")),
        name=name,
    )(be, nu, x, w1, w3)


GLA_NCH = RB // CHUNK


def _gla_kernel(q_ref, k_ref, v_ref, r_ref, al_ref, wau_ref, ba_ref, gn_ref, o_ref, st_ref):
    s = pl.program_id(2)

    @pl.when(s == 0)
    def _():
        st_ref[...] = jnp.zeros_like(st_ref)

    row = lax.broadcasted_iota(I32, (CHUNK, CHUNK), 0)
    col = lax.broadcasted_iota(I32, (CHUNK, CHUNK), 1)
    causal = row >= col
    tril = jnp.where(causal, 1.0, 0.0).astype(BF16)
    rowid = lax.broadcasted_iota(I32, (CHUNK, 1), 0)
    wau = wau_ref[...]
    ba = ba_ref[...]
    gn = gn_ref[...]
    nt = (((1,), (1,)), ((), ()))
    tn = (((0,), (0,)), ((), ()))

    for ci in range(GLA_NCH):
        rs = pl.ds(ci * CHUNK, CHUNK)
        if ci == 0:
            valid = jnp.logical_or(s > 0, rowid < N_META)
        else:
            valid = jnp.logical_and(s > 0, rowid >= 0)
        x = jnp.dot(al_ref[rs, :].astype(BF16), wau, preferred_element_type=F32) + ba
        la = (jnp.minimum(x, 0.0) - jnp.log1p(jnp.exp(-jnp.abs(x)))) * (1.0 / TAU)
        la = jnp.where(valid, la, 0.0)
        la_hi = la.astype(BF16)
        la_lo = (la - la_hi.astype(F32)).astype(BF16)
        bcum = (jnp.dot(tril, la_hi, preferred_element_type=F32)
                + jnp.dot(tril, la_lo, preferred_element_type=F32))
        blast = bcum[CHUNK - 1:CHUNK, :]
        q = q_ref[rs, :].astype(F32) * (DKH ** -0.5)
        k = jnp.where(valid, k_ref[rs, :].astype(F32), 0.0)
        v = jnp.where(valid, v_ref[rs, :], jnp.zeros((), BF16))
        qd = (q * jnp.exp(bcum)).astype(BF16)
        kd = (k * jnp.exp(-bcum)).astype(BF16)
        ke = (k * jnp.exp(blast - bcum)).astype(BF16)
        dec = jnp.exp(blast)
        sc = lax.dot_general(qd, kd, nt, preferred_element_type=F32)
        sc = jnp.where(causal, sc, 0.0).astype(BF16)
        st = st_ref[...]
        o = (jnp.dot(sc, v, preferred_element_type=F32)
             + lax.dot_general(qd, st.astype(BF16), nt, preferred_element_type=F32))
        st_ref[...] = dec * st + lax.dot_general(v, ke, tn, preferred_element_type=F32)
        ms = jnp.mean(o * o, axis=-1, keepdims=True)
        on = o * lax.rsqrt(ms + EPS) * gn
        r = r_ref[rs, :].astype(F32)
        o_ref[rs, :] = ((r * _sigmoid(r)) * on).astype(BF16)


def _gla_rowblk(b, s):
    return jnp.where(s == 0, META_BLK, b * BLK_PER_SEQ + s - 1)


def _out_rowblk(b, s):
    return jnp.where(s == 0, META_BLK + jnp.minimum(b, 1), b * BLK_PER_SEQ + s - 1)


def gla(p1, alow, wau, ba, gn):
    nq = DK // DKH
    return pl.pallas_call(
        _gla_kernel,
        grid=(BATCH, HEADS, 1 + BLK_PER_SEQ),
        in_specs=[
            pl.BlockSpec((RB, DKH), lambda b, h, s: (_gla_rowblk(b, s), h)),
            pl.BlockSpec((RB, DKH), lambda b, h, s: (_gla_rowblk(b, s), nq + h)),
            pl.BlockSpec((RB, DVH), lambda b, h, s: (_gla_rowblk(b, s), 2 * DK // DVH + h)),
            pl.BlockSpec((RB, DVH), lambda b, h, s: (_gla_rowblk(b, s), (2 * DK + DV) // DVH + h)),
            pl.BlockSpec((RB, 128), lambda b, h, s: (_gla_rowblk(b, s), 0)),
            pl.BlockSpec((128, DKH), lambda b, h, s: (0, h)),
            pl.BlockSpec((1, DKH), lambda b, h, s: (0, h)),
            pl.BlockSpec((1, DVH), lambda b, h, s: (0, h)),
        ],
        out_specs=pl.BlockSpec((RB, DVH), lambda b, h, s: (_out_rowblk(b, s), h)),
        out_shape=jax.ShapeDtypeStruct((R + RB, DV), BF16),
        scratch_shapes=[pltpu.VMEM((DVH, DKH), F32)],
        compiler_params=_cparams(("arbitrary", "arbitrary", "arbitrary")),
        name="gla",
    )(p1, p1, p1, p1, alow, wau, ba, gn)


HALO = 32
CONV_RC = 64
LANES = 128


def _conv_kernel(a_ref, g_ref, cw_ref, cb_ref, ng_ref, nb_ref, y_ref, ubuf, ybuf):
    s = pl.program_id(1)
    u = a_ref[...].astype(F32) * _sigmoid(g_ref[...].astype(F32))

    @pl.when(s == 0)
    def _():
        rowid = lax.broadcasted_iota(I32, (RB, 1), 0)
        ubuf[0:HALO, :] = jnp.zeros((HALO, D), F32)
        ubuf[HALO:HALO + RB, :] = jnp.where(rowid < N_META, u, 0.0)

    @pl.when(s > 0)
    def _():
        ubuf[HALO:HALO + RB, :] = u

    def lane_chunk(lc, carry):
        ls = pl.ds(pl.multiple_of(lc * LANES, LANES), LANES)
        for rc in range(RB // CONV_RC):
            acc = jnp.zeros((CONV_RC, LANES), F32)
            for j in range(CONV_W):
                off = rc * CONV_RC + HALO - (CONV_W - 1) + j
                acc = acc + cw_ref[j:j + 1, ls] * ubuf[off:off + CONV_RC, ls]
            ybuf[rc * CONV_RC:(rc + 1) * CONV_RC, ls] = acc + cb_ref[:, ls]
        return carry

    lax.fori_loop(0, D // LANES, lane_chunk, 0)

    @pl.when(s == 0)
    def _():
        ubuf[0:HALO, :] = ubuf[N_META:N_META + HALO, :]

    @pl.when(s > 0)
    def _():
        ubuf[0:HALO, :] = ubuf[RB:RB + HALO, :]

    y = _layer_norm(ybuf[...], ng_ref[...], nb_ref[...])
    y_ref[...] = (y * _sigmoid(y)).astype(BF16)


def _conv_rowblk(b, s):
    return jnp.where(s == 0, META_BLK, b * BLK_PER_SEQ + s - 1)


def conv_module(p2, cw, cb, ng, nb):
    cwp = jnp.concatenate([cw, jnp.zeros((1, D), F32)], axis=0)
    return pl.pallas_call(
        _conv_kernel,
        grid=(BATCH, 1 + BLK_PER_SEQ),
        in_specs=[
            pl.BlockSpec((RB, D), lambda b, s: (_conv_rowblk(b, s), 0)),
            pl.BlockSpec((RB, D), lambda b, s: (_conv_rowblk(b, s), 1)),
            pl.BlockSpec((CONV_W + 1, D), lambda b, s: (0, 0)),
            pl.BlockSpec((1, D), lambda b, s: (0, 0)),
            pl.BlockSpec((1, D), lambda b, s: (0, 0)),
            pl.BlockSpec((1, D), lambda b, s: (0, 0)),
        ],
        out_specs=pl.BlockSpec((RB, D), lambda b, s: (_out_rowblk(b, s), 0)),
        out_shape=jax.ShapeDtypeStruct((R + RB, D), BF16),
        scratch_shapes=[pltpu.VMEM((HALO + RB, D), F32), pltpu.VMEM((RB, D), F32)],
        compiler_params=_cparams(("arbitrary", "arbitrary")),
        name="conv",
    )(p2, p2, cwp, cb.reshape(1, D), ng.reshape(1, D), nb.reshape(1, D))


def _merge_kernel(oa_ref, yc_ref, ga_ref, gb_ref, wg_ref, wc_ref, m_ref):
    ya = jnp.dot(oa_ref[...], wg_ref[...], preferred_element_type=F32)
    yb = jnp.dot(yc_ref[...], wc_ref[...], preferred_element_type=F32)
    m = _sigmoid(ga_ref[...].astype(F32)) * ya + _sigmoid(gb_ref[...].astype(F32)) * yb
    m_ref[...] = m.astype(BF16)


def merge(oa, yc, p2, wg, wc):
    wspec = pl.BlockSpec((D, D), lambda i: (0, 0), pipeline_mode=pl.Buffered(1))
    return pl.pallas_call(
        _merge_kernel,
        grid=(NRB,),
        in_specs=[pl.BlockSpec((RB, D), lambda i: (i, 0)),
                  pl.BlockSpec((RB, D), lambda i: (i, 0)),
                  pl.BlockSpec((RB, D), lambda i: (i, 2)),
                  pl.BlockSpec((RB, D), lambda i: (i, 3)),
                  wspec, wspec],
        out_specs=pl.BlockSpec((RB, D), lambda i: (i, 0)),
        out_shape=jax.ShapeDtypeStruct((R, D), BF16),
        compiler_params=_cparams(("arbitrary",)),
        name="merge",
    )(oa, yc, p2, p2, wg, wc)


def _out_ln_kernel(m_ref, h_ref, wo_ref, g_ref, b_ref, ho_ref, hb_ref):
    mix = jnp.dot(m_ref[...], wo_ref[...], preferred_element_type=F32)
    h = _layer_norm(ALPHA * h_ref[...] + mix, g_ref[...], b_ref[...])
    ho_ref[...] = h
    hb_ref[...] = h.astype(BF16)


def out_ln(m, h, wo, g, b):
    return pl.pallas_call(
        _out_ln_kernel,
        grid=(NRB,),
        in_specs=[pl.BlockSpec((RB, D), lambda i: (i, 0)),
                  pl.BlockSpec((RB, D), lambda i: (i, 0)),
                  pl.BlockSpec((D, D), lambda i: (0, 0), pipeline_mode=pl.Buffered(1)),
                  pl.BlockSpec((1, D), lambda i: (0, 0)),
                  pl.BlockSpec((1, D), lambda i: (0, 0))],
        out_specs=[pl.BlockSpec((RB, D), lambda i: (i, 0)),
                   pl.BlockSpec((RB, D), lambda i: (i, 0))],
        out_shape=[jax.ShapeDtypeStruct((R, D), F32), jax.ShapeDtypeStruct((R, D), BF16)],
        compiler_params=_cparams(("arbitrary",)),
        name="out_ln",
    )(m, h, wo, g.reshape(1, D), b.reshape(1, D))


def _router_kernel(h_ref, rw_ref, rb_ref, route_ref, cnt_ref, carry_ref):
    i = pl.program_id(0)

    @pl.when(i == 0)
    def _():
        carry_ref[...] = jnp.zeros_like(carry_ref)

    logits = jnp.dot(h_ref[...], rw_ref[...], preferred_element_type=F32,
                     precision=lax.Precision.HIGHEST) + rb_ref[...]
    lane = lax.broadcasted_iota(I32, (RB, LANES), 1).astype(F32)
    neg = jnp.float32(-jnp.inf)
    lg = jnp.where(lane < NE, logits, neg)
    m1 = jnp.max(lg, axis=1, keepdims=True)
    i1 = jnp.min(jnp.where(lg == m1, lane, float(LANES)), axis=1, keepdims=True)
    lg2 = jnp.where(lane == i1, neg, lg)
    m2 = jnp.max(lg2, axis=1, keepdims=True)
    i2 = jnp.min(jnp.where(lg2 == m2, lane, float(LANES)), axis=1, keepdims=True)
    e = jnp.exp(m2 - m1)
    g1 = 1.0 / (1.0 + e)
    g2 = e / (1.0 + e)

    rowid = i * RB + lax.broadcasted_iota(I32, (RB, 1), 0)
    is_tok = rowid < N_TOK
    hit1 = lane == i1
    hit2 = lane == i2
    onehot = jnp.where(jnp.logical_and(jnp.logical_or(hit1, hit2), is_tok), 1.0, 0.0)
    r_i = lax.broadcasted_iota(I32, (RB, RB), 0)
    c_i = lax.broadcasted_iota(I32, (RB, RB), 1)
    strict = jnp.where(r_i > c_i, 1.0, 0.0).astype(BF16)
    before = jnp.dot(strict, onehot.astype(BF16), preferred_element_type=F32) + carry_ref[0:1, :]
    pos1 = jnp.sum(jnp.where(hit1, before, 0.0), axis=1, keepdims=True)
    pos2 = jnp.sum(jnp.where(hit2, before, 0.0), axis=1, keepdims=True)
    carry_ref[...] = carry_ref[...] + jnp.sum(onehot, axis=0, keepdims=True)

    out = jnp.where(lane == 0, i1, 0.0)
    out = jnp.where(lane == 1, i2, out)
    out = jnp.where(lane == 2, pos1, out)
    out = jnp.where(lane == 3, pos2, out)
    out = jnp.where(lane == 4, g1, out)
    out = jnp.where(lane == 5, g2, out)
    route_ref[...] = out
    cnt_ref[...] = carry_ref[...]


def router(h, rw, rb):
    rwp = jnp.zeros((D, LANES), F32).at[:, :NE].set(rw)
    rbp = jnp.zeros((1, LANES), F32).at[0, :NE].set(rb)
    return pl.pallas_call(
        _router_kernel,
        grid=(NRB,),
        in_specs=[pl.BlockSpec((RB, D), lambda i: (i, 0)),
                  pl.BlockSpec((D, LANES), lambda i: (0, 0)),
                  pl.BlockSpec((1, LANES), lambda i: (0, 0))],
        out_specs=[pl.BlockSpec((RB, LANES), lambda i: (i, 0)),
                   pl.BlockSpec((8, LANES), lambda i: (0, 0))],
        out_shape=[jax.ShapeDtypeStruct((R, LANES), F32), jax.ShapeDtypeStruct((8, LANES), F32)],
        scratch_shapes=[pltpu.VMEM((8, LANES), F32)],
        compiler_params=_cparams(("arbitrary",)),
        name="router",
    )(h, rwp, rbp)


DISP_BATCH = 64


def _dispatch_kernel(d1_ref, d2_ref, h_ref, xs_in_ref, xs_ref, sem):
    del xs_in_ref

    def burst(t0, carry):
        base = t0 * DISP_BATCH

        def copies(t):
            row = h_ref.at[pl.ds(base + t, 1)]
            c1 = pltpu.make_async_copy(row, xs_ref.at[pl.ds(d1_ref[base + t], 1)], sem.at[0])
            c2 = pltpu.make_async_copy(row, xs_ref.at[pl.ds(d2_ref[base + t], 1)], sem.at[1])
            return c1, c2

        def start(t, c):
            c1, c2 = copies(t)
            c1.start()
            c2.start()
            return c

        def wait(t, c):
            c1, c2 = copies(t)
            c1.wait()
            c2.wait()
            return c

        n = jnp.minimum(DISP_BATCH, N_TOK - base)
        lax.fori_loop(0, n, start, 0)
        lax.fori_loop(0, n, wait, 0)
        return carry

    lax.fori_loop(0, -(-N_TOK // DISP_BATCH), burst, 0)


def dispatch(h, d1, d2):
    xs0 = jnp.zeros((N_SLOTS, D), F32)
    return pl.pallas_call(
        _dispatch_kernel,
        grid_spec=pltpu.PrefetchScalarGridSpec(
            num_scalar_prefetch=2,
            grid=(1,),
            in_specs=[pl.BlockSpec(memory_space=pl.ANY), pl.BlockSpec(memory_space=pl.ANY)],
            out_specs=pl.BlockSpec(memory_space=pl.ANY),
            scratch_shapes=[pltpu.SemaphoreType.DMA((2,))]),
        out_shape=jax.ShapeDtypeStruct((N_SLOTS, D), F32),
        input_output_aliases={3: 0},
        compiler_params=_cparams(("arbitrary",)),
        name="dispatch",
    )(d1, d2, h, xs0)


def _combine_kernel(d1_ref, d2_ref, y_ref, h_ref, route_ref, g_ref, b_ref, ho_ref, hb_ref,
                    buf1, buf2, sem):
    i = pl.program_id(0)
    base = i * RB

    def copies(t):
        c1 = pltpu.make_async_copy(y_ref.at[pl.ds(d1_ref[base + t], 1)], buf1.at[pl.ds(t, 1)], sem.at[0])
        c2 = pltpu.make_async_copy(y_ref.at[pl.ds(d2_ref[base + t], 1)], buf2.at[pl.ds(t, 1)], sem.at[1])
        return c1, c2

    def start(t, c):
        c1, c2 = copies(t)
        c1.start()
        c2.start()
        return c

    def wait(t, c):
        c1, c2 = copies(t)
        c1.wait()
        c2.wait()
        return c

    lax.fori_loop(0, RB, start, 0)
    lax.fori_loop(0, RB, wait, 0)
    route = route_ref[...]
    g1 = route[:, 4:5]
    g2 = route[:, 5:6]
    f = g1 * buf1[...] + g2 * buf2[...]
    h = _layer_norm(ALPHA * h_ref[...] + f, g_ref[...], b_ref[...])
    ho_ref[...] = h
    hb_ref[...] = h.astype(BF16)


def combine(y, h, route, d1, d2, g, b):
    return pl.pallas_call(
        _combine_kernel,
        grid_spec=pltpu.PrefetchScalarGridSpec(
            num_scalar_prefetch=2,
            grid=(NRB,),
            in_specs=[pl.BlockSpec(memory_space=pl.ANY),
                      pl.BlockSpec((RB, D), lambda i, d1, d2: (i, 0)),
                      pl.BlockSpec((RB, LANES), lambda i, d1, d2: (i, 0)),
                      pl.BlockSpec((1, D), lambda i, d1, d2: (0, 0)),
                      pl.BlockSpec((1, D), lambda i, d1, d2: (0, 0))],
            out_specs=[pl.BlockSpec((RB, D), lambda i, d1, d2: (i, 0)),
                       pl.BlockSpec((RB, D), lambda i, d1, d2: (i, 0))],
            scratch_shapes=[pltpu.VMEM((RB, D), F32), pltpu.VMEM((RB, D), F32),
                            pltpu.SemaphoreType.DMA((2,))]),
        out_shape=[jax.ShapeDtypeStruct((R, D), F32), jax.ShapeDtypeStruct((R, D), BF16)],
        compiler_params=_cparams(("arbitrary",)),
        name="combine",
    )(d1, d2, y, h, route, g.reshape(1, D), b.reshape(1, D))


def _const_sched(nblk, e):
    return jnp.full((nblk,), e, I32), jnp.full((1,), nblk, I32)


def mixer_layer(i, h, hb, w_in, w_rest, w_alpha_up, b_alpha, gla_norm_g, w_gla_o, conv_w, conv_b,
                conv_norm_g, conv_norm_b, w_conv_o, w_out, ln_g, ln_b):
    be, nu = _const_sched(R // BIG, i)
    p1 = gmm(hb, w_in, be, nu, blk=BIG, tn=512, n_out=COL_ALOW, name="in_qkvr")
    alow = gmm(hb, w_in, be, nu, blk=BIG, tn=128, n_out=128, col_off=COL_ALOW // 128,
               out_dtype=F32, name="in_alow")
    p2 = gmm(hb, w_rest, be, nu, blk=BIG, tn=512, n_out=4 * D, name="in_rest")
    wau = jnp.zeros((128, DK), BF16).at[:LOWRANK].set(w_alpha_up[i].astype(BF16))
    oa = gla(p1, alow, wau, b_alpha[i].reshape(1, DK), gla_norm_g[i].reshape(1, DV))
    yc = conv_module(p2, conv_w[i], conv_b[i], conv_norm_g[i], conv_norm_b[i])
    m = merge(oa, yc, p2, w_gla_o[i].astype(BF16), w_conv_o[i].astype(BF16))
    return out_ln(m, h, w_out[i].astype(BF16), ln_g[i], ln_b[i])


def dense_ffn_layer(j, h, hb, w1, w3, w2, ln_g, ln_b):
    be, nu = _const_sched(R // BIG, j)
    g = gateup(hb, w1, w3, be, nu, blk=BIG, tn=512, name="ffn_gateup")
    be2, nu2 = _const_sched(NRB, j)
    y = gmm(g, w2, be2, nu2, blk=RB, tn=512, n_out=D, out_dtype=F32, name="ffn_down")
    return res_ln(h, y, ln_g, ln_b)


def moe_ffn_layer(j, h, hb, router_w, router_b, w1, w3, w2, ln_g, ln_b):
    del hb
    route, cnt = router(h, router_w[j], router_b[j])
    counts = cnt[0, :NE].astype(I32)
    nblk_e = (counts + MOE_BLK - 1) // MOE_BLK
    blk_end = jnp.cumsum(nblk_e)
    blk_start = blk_end - nblk_e
    n_used = blk_end[NE - 1]
    blk_id = jnp.minimum(jnp.arange(MOE_NBLK, dtype=I32), n_used - 1)
    blk_exp = jnp.minimum(jnp.searchsorted(blk_end, blk_id, side="right"), NE - 1).astype(I32)
    e1 = route[:, 0].astype(I32)
    e2 = route[:, 1].astype(I32)
    is_tok = jnp.arange(R, dtype=I32) < N_TOK
    d1 = jnp.where(is_tok, blk_start[e1] * MOE_BLK + route[:, 2].astype(I32), 0)
    d2 = jnp.where(is_tok, blk_start[e2] * MOE_BLK + route[:, 3].astype(I32), 0)
    xs = dispatch(h, d1, d2)
    be = blk_exp + j * NE
    nu = n_used.reshape(1)
    w1f = w1.reshape(-1, D, DFF)
    w3f = w3.reshape(-1, D, DFF)
    w2f = w2.reshape(-1, DFF, D)
    g = gateup(xs, w1f, w3f, be, nu, blk=MOE_BLK, tn=512, name="moe_gateup")
    y = gmm(g, w2f, be, nu, blk=MOE_BLK, tn=512, n_out=D, out_dtype=F32, name="moe_down")
    return combine(y, h, route, d1, d2, ln_g, ln_b)


def kernel(x, meta_tokens, ln_in_g, ln_in_b, w_in, w_alpha_up, b_alpha, gla_norm_g, w_gla_o, conv_w, conv_b, conv_norm_g, conv_norm_b, w_conv_o, w_out, ln_mix_g, ln_mix_b, ffn_w1, ffn_w3, ffn_w2, router_w, router_b, moe_w1, moe_w3, moe_w2, ln_ffn_g, ln_ffn_b):
    rows = jnp.concatenate([x.reshape(R_MAIN, D), meta_tokens,
                            jnp.zeros((R - N_TOK, D), F32)], axis=0)
    h, hb = ln_in(rows, ln_in_g, ln_in_b)
    w_rest = w_in[:, :, COL_REST:]
    for i in range(DEPTH):
        h, hb = mixer_layer(i, h, hb, w_in, w_rest, w_alpha_up, b_alpha, gla_norm_g, w_gla_o,
                            conv_w, conv_b, conv_norm_g, conv_norm_b, w_conv_o, w_out,
                            ln_mix_g, ln_mix_b)
        j = i // 2
        if i % 2 == 0:
            h, hb = dense_ffn_layer(j, h, hb, ffn_w1, ffn_w3, ffn_w2, ln_ffn_g[i], ln_ffn_b[i])
        else:
            h, hb = moe_ffn_layer(j, h, hb, router_w, router_b, moe_w1, moe_w3, moe_w2,
                                  ln_ffn_g[i], ln_ffn_b[i])
    return h[:R_MAIN].reshape(BATCH, SEQ, D)
```

```python
import functools

import jax
import jax.numpy as jnp
from jax import lax
from jax.experimental import pallas as pl
from jax.experimental.pallas import tpu as pltpu

F32 = jnp.float32
BF16 = jnp.bfloat16
I32 = jnp.int32

D = 2048
BATCH = 2
SEQ = 4096
DEPTH = 4
N_META = 16
HEADS = 4
DK = 1024
DV = 2048
DKH = DK // HEADS
DVH = DV // HEADS
LOWRANK = 16
TAU = 16.0
CHUNK = 64
CONV_W = 31
DFF = 5632
NE = 8
ALPHA = (2.0 * DEPTH) ** 0.25
EPS = 1e-5
D_IN = 14352
COL_ALOW = 6144
COL_REST = 6160

R_MAIN = BATCH * SEQ
META0 = R_MAIN
N_TOK = R_MAIN + N_META
RB = 256
R = 8448
NRB = R // RB
META_BLK = META0 // RB
BLK_PER_SEQ = SEQ // RB
BIG = 768

MOE_BLK = 512
N_ASSIGN = 2 * N_TOK
MOE_NBLK = -(-N_ASSIGN // MOE_BLK) + NE
N_SLOTS = MOE_NBLK * MOE_BLK

VMEM_LIMIT = 56 * 1024 * 1024


def _cparams(sem):
    return pltpu.CompilerParams(dimension_semantics=sem, vmem_limit_bytes=VMEM_LIMIT)


def _sigmoid(x):
    return 0.5 * jnp.tanh(0.5 * x) + 0.5


def _pack_halves(h):
    half = h.shape[1] // 2
    lo = lax.bitcast_convert_type(h[:, :half].astype(BF16).astype(F32), jnp.uint32)
    hi = lax.bitcast_convert_type(h[:, half:].astype(BF16).astype(F32), jnp.uint32)
    return (hi & jnp.uint32(0xFFFF0000)) | (lo >> jnp.uint32(16))


def _unpack_halves(p):
    lo = lax.bitcast_convert_type(p << jnp.uint32(16), F32).astype(BF16)
    hi = lax.bitcast_convert_type(p & jnp.uint32(0xFFFF0000), F32).astype(BF16)
    return jnp.concatenate([lo, hi], axis=1)


def _layer_norm(z, g, b):
    mu = jnp.mean(z, axis=-1, keepdims=True)
    zc = z - mu
    var = jnp.mean(zc * zc, axis=-1, keepdims=True)
    return zc * lax.rsqrt(var + EPS) * g + b


def _ln_in_kernel(x_ref, meta_ref, g_ref, b_ref, h_ref, hb_ref):
    def emit(rows):
        h = _layer_norm(rows, g_ref[...], b_ref[...])
        h_ref[...] = h
        hb_ref[...] = h.astype(BF16)

    @pl.when(pl.program_id(0) < META_BLK)
    def _():
        emit(x_ref[...])

    @pl.when(pl.program_id(0) == META_BLK)
    def _():
        emit(meta_ref[...])


def ln_in(x, meta_blk, g, b):
    return pl.pallas_call(
        _ln_in_kernel,
        grid=(NRB,),
        in_specs=[pl.BlockSpec((RB, D), lambda i: (jnp.minimum(i, META_BLK - 1), 0)),
                  pl.BlockSpec((RB, D), lambda i: (0, 0)),
                  pl.BlockSpec((1, D), lambda i: (0, 0)),
                  pl.BlockSpec((1, D), lambda i: (0, 0))],
        out_specs=[pl.BlockSpec((RB, D), lambda i: (i, 0)),
                   pl.BlockSpec((RB, D), lambda i: (i, 0))],
        out_shape=[jax.ShapeDtypeStruct((R, D), F32), jax.ShapeDtypeStruct((R, D), BF16)],
        compiler_params=_cparams(("arbitrary",)),
        name="ln_in",
    )(x, meta_blk, g.reshape(1, D), b.reshape(1, D))


def _res_ln_kernel(h_ref, f_ref, g_ref, b_ref, ho_ref, hb_ref):
    z = ALPHA * h_ref[...] + f_ref[...]
    h = _layer_norm(z, g_ref[...], b_ref[...])
    ho_ref[...] = h
    hb_ref[...] = h.astype(BF16)


def res_ln(h, f, g, b):
    return pl.pallas_call(
        _res_ln_kernel,
        grid=(NRB,),
        in_specs=[pl.BlockSpec((RB, D), lambda i: (i, 0)),
                  pl.BlockSpec((RB, D), lambda i: (i, 0)),
                  pl.BlockSpec((1, D), lambda i: (0, 0)),
                  pl.BlockSpec((1, D), lambda i: (0, 0))],
        out_specs=[pl.BlockSpec((RB, D), lambda i: (i, 0)),
                   pl.BlockSpec((RB, D), lambda i: (i, 0))],
        out_shape=[jax.ShapeDtypeStruct((R, D), F32), jax.ShapeDtypeStruct((R, D), BF16)],
        compiler_params=_cparams(("arbitrary",)),
        name="res_ln",
    )(h, f, g.reshape(1, D), b.reshape(1, D))


def _weight_changed(be_ref, b):
    prev = be_ref[jnp.maximum(b - 1, 0)]
    return jnp.logical_or(b == 0, be_ref[b] != prev)


def _gmm_kernel(be_ref, nu_ref, x_ref, w_ref, o_ref, wb_ref):
    b = pl.program_id(1)

    @pl.when(_weight_changed(be_ref, b))
    def _():
        wb_ref[...] = w_ref[...].astype(BF16)

    @pl.when(b < nu_ref[0])
    def _():
        x = x_ref[...].astype(BF16)
        o_ref[...] = jnp.dot(x, wb_ref[...], preferred_element_type=F32).astype(o_ref.dtype)

    @pl.when(b >= nu_ref[0])
    def _():
        o_ref[...] = jnp.zeros_like(o_ref)


def gmm(x, w, be, nu, *, blk, tn, n_out, col_off=0, out_dtype=BF16, name="gmm"):
    rows, k = x.shape
    grid = (n_out // tn, rows // blk)
    return pl.pallas_call(
        _gmm_kernel,
        grid_spec=pltpu.PrefetchScalarGridSpec(
            num_scalar_prefetch=2,
            grid=grid,
            in_specs=[pl.BlockSpec((blk, k), lambda j, b, be, nu: (jnp.minimum(b, nu[0] - 1), 0)),
                      pl.BlockSpec((None, k, tn), lambda j, b, be, nu: (be[b], 0, j + col_off))],
            out_specs=pl.BlockSpec((blk, tn), lambda j, b, be, nu: (b, j)),
            scratch_shapes=[pltpu.VMEM((k, tn), BF16)]),
        out_shape=jax.ShapeDtypeStruct((rows, n_out), out_dtype),
        compiler_params=_cparams(("arbitrary", "arbitrary")),
        name=name,
    )(be, nu, x, w)


SHIFT = COL_REST - COL_ALOW


def _gmm_shift_kernel(x_ref, w_ref, wn_ref, o_ref, wb_ref):
    @pl.when(pl.program_id(1) == 0)
    def _():
        w = jnp.concatenate([w_ref[:, SHIFT:], wn_ref[:, :SHIFT]], axis=1)
        wb_ref[...] = w.astype(BF16)

    o_ref[...] = jnp.dot(x_ref[...], wb_ref[...], preferred_element_type=F32).astype(o_ref.dtype)


def gmm_shift(x, w, layer, *, blk, tn, n_out, name="gmm_shift"):
    rows, k = x.shape
    grid = (n_out // tn, rows // blk)
    c0 = COL_ALOW // tn
    return pl.pallas_call(
        _gmm_shift_kernel,
        grid=grid,
        in_specs=[pl.BlockSpec((blk, k), lambda j, b: (b, 0)),
                  pl.BlockSpec((None, k, tn), lambda j, b: (layer, 0, c0 + j)),
                  pl.BlockSpec((None, k, LANES), lambda j, b: (layer, 0, (c0 + j + 1) * (tn // LANES)))],
        out_specs=pl.BlockSpec((blk, tn), lambda j, b: (b, j)),
        out_shape=jax.ShapeDtypeStruct((rows, n_out), BF16),
        scratch_shapes=[pltpu.VMEM((k, tn), BF16)],
        compiler_params=_cparams(("arbitrary", "arbitrary")),
        name=name,
    )(x, w, w)


def _gateup_kernel(be_ref, nu_ref, x_ref, w1_ref, w3_ref, o_ref, w1b_ref, w3b_ref):
    b = pl.program_id(1)

    @pl.when(_weight_changed(be_ref, b))
    def _():
        w1b_ref[...] = w1_ref[...].astype(BF16)
        w3b_ref[...] = w3_ref[...].astype(BF16)

    @pl.when(b < nu_ref[0])
    def _():
        if x_ref.dtype == jnp.uint32:
            x = _unpack_halves(x_ref[...])
        else:
            x = x_ref[...]
        a = jnp.dot(x, w1b_ref[...], preferred_element_type=F32)
        c = jnp.dot(x, w3b_ref[...], preferred_element_type=F32)
        o_ref[...] = ((a * _sigmoid(a)) * c).astype(o_ref.dtype)

    @pl.when(b >= nu_ref[0])
    def _():
        o_ref[...] = jnp.zeros_like(o_ref)


def gateup(x, w1, w3, be, nu, *, blk, tn, name="gateup"):
    rows, kx = x.shape
    k, n_out = w1.shape[-2:]
    grid = (n_out // tn, rows // blk)
    wspec = pl.BlockSpec((None, k, tn), lambda j, b, be, nu: (be[b], 0, j))
    return pl.pallas_call(
        _gateup_kernel,
        grid_spec=pltpu.PrefetchScalarGridSpec(
            num_scalar_prefetch=2,
            grid=grid,
            in_specs=[pl.BlockSpec((blk, kx), lambda j, b, be, nu: (jnp.minimum(b, nu[0] - 1), 0)),
                      wspec, wspec],
            out_specs=pl.BlockSpec((blk, tn), lambda j, b, be, nu: (b, j)),
            scratch_shapes=[pltpu.VMEM((k, tn), BF16), pltpu.VMEM((k, tn), BF16)]),
        out_shape=jax.ShapeDtypeStruct((rows, n_out), BF16),
        compiler_params=_cparams(("arbitrary", "arbitrary")),
        name=name,
    )(be, nu, x, w1, w3)


GLA_NCH = RB // CHUNK


def _gla_kernel(q_ref, k_ref, v_ref, r_ref, al_ref, wau_ref, ba_ref, gn_ref, o_ref, st_ref):
    s = pl.program_id(2)

    @pl.when(s == 0)
    def _():
        st_ref[...] = jnp.zeros_like(st_ref)

    row = lax.broadcasted_iota(I32, (CHUNK, CHUNK), 0)
    col = lax.broadcasted_iota(I32, (CHUNK, CHUNK), 1)
    causal = row >= col
    tril = jnp.where(causal, 1.0, 0.0).astype(BF16)
    rowid = lax.broadcasted_iota(I32, (CHUNK, 1), 0)
    wau = wau_ref[...]
    ba = ba_ref[...]
    gn = gn_ref[...]
    nt = (((1,), (1,)), ((), ()))
    tn = (((0,), (0,)), ((), ()))

    for ci in range(GLA_NCH):
        rs = pl.ds(ci * CHUNK, CHUNK)
        if ci == 0:
            valid = jnp.logical_or(s > 0, rowid < N_META)
        else:
            valid = jnp.logical_and(s > 0, rowid >= 0)
        x = jnp.dot(al_ref[rs, :].astype(BF16), wau, preferred_element_type=F32) + ba
        la = (jnp.minimum(x, 0.0) - jnp.log1p(jnp.exp(-jnp.abs(x)))) * (1.0 / TAU)
        la = jnp.where(valid, la, 0.0)
        la_hi = la.astype(BF16)
        la_lo = (la - la_hi.astype(F32)).astype(BF16)
        bcum = (jnp.dot(tril, la_hi, preferred_element_type=F32)
                + jnp.dot(tril, la_lo, preferred_element_type=F32))
        blast = bcum[CHUNK - 1:CHUNK, :]
        q = q_ref[rs, :].astype(F32) * (DKH ** -0.5)
        k = jnp.where(valid, k_ref[rs, :].astype(F32), 0.0)
        v = jnp.where(valid, v_ref[rs, :], jnp.zeros((), BF16))
        qd = (q * jnp.exp(bcum)).astype(BF16)
        kd = (k * jnp.exp(-bcum)).astype(BF16)
        ke = (k * jnp.exp(blast - bcum)).astype(BF16)
        dec = jnp.exp(blast)
        sc = lax.dot_general(qd, kd, nt, preferred_element_type=F32)
        sc = jnp.where(causal, sc, 0.0).astype(BF16)
        st = st_ref[...]
        o = (jnp.dot(sc, v, preferred_element_type=F32)
             + lax.dot_general(qd, st.astype(BF16), nt, preferred_element_type=F32))
        st_ref[...] = dec * st + lax.dot_general(v, ke, tn, preferred_element_type=F32)
        ms = jnp.mean(o * o, axis=-1, keepdims=True)
        on = o * lax.rsqrt(ms + EPS) * gn
        r = r_ref[rs, :].astype(F32)
        o_ref[rs, :] = ((r * _sigmoid(r)) * on).astype(BF16)


def _gla_rowblk(b, s):
    return jnp.where(s == 0, META_BLK, b * BLK_PER_SEQ + s - 1)


def _out_rowblk(b, s):
    return jnp.where(s == 0, META_BLK + jnp.minimum(b, 1), b * BLK_PER_SEQ + s - 1)


def gla(p1, alow, wau, ba, gn):
    nq = DK // DKH
    return pl.pallas_call(
        _gla_kernel,
        grid=(BATCH, HEADS, 1 + BLK_PER_SEQ),
        in_specs=[
            pl.BlockSpec((RB, DKH), lambda b, h, s: (_gla_rowblk(b, s), h)),
            pl.BlockSpec((RB, DKH), lambda b, h, s: (_gla_rowblk(b, s), nq + h)),
            pl.BlockSpec((RB, DVH), lambda b, h, s: (_gla_rowblk(b, s), 2 * DK // DVH + h)),
            pl.BlockSpec((RB, DVH), lambda b, h, s: (_gla_rowblk(b, s), (2 * DK + DV) // DVH + h)),
            pl.BlockSpec((RB, 128), lambda b, h, s: (_gla_rowblk(b, s), 0)),
            pl.BlockSpec((128, DKH), lambda b, h, s: (0, h)),
            pl.BlockSpec((1, DKH), lambda b, h, s: (0, h)),
            pl.BlockSpec((1, DVH), lambda b, h, s: (0, h)),
        ],
        out_specs=pl.BlockSpec((RB, DVH), lambda b, h, s: (_out_rowblk(b, s), h)),
        out_shape=jax.ShapeDtypeStruct((R + RB, DV), BF16),
        scratch_shapes=[pltpu.VMEM((DVH, DKH), F32)],
        compiler_params=_cparams(("arbitrary", "arbitrary", "arbitrary")),
        name="gla",
    )(p1, p1, p1, p1, alow, wau, ba, gn)


HALO = 32
CONV_RC = 128
LANES = 128
SUBLANES = 8
CONV_BACK = 24


def _conv_kernel(a_ref, g_ref, cw_ref, cb_ref, ng_ref, nb_ref, y_ref, ubuf, ybuf):
    s = pl.program_id(1)
    u = a_ref[...].astype(F32) * _sigmoid(g_ref[...].astype(F32))

    @pl.when(s == 0)
    def _():
        rowid = lax.broadcasted_iota(I32, (RB, 1), 0)
        ubuf[0:HALO, :] = jnp.zeros((HALO, D), F32)
        ubuf[HALO:HALO + RB, :] = jnp.where(rowid < N_META, u, 0.0)

    @pl.when(s > 0)
    def _():
        ubuf[HALO:HALO + RB, :] = u

    def lane_chunk(lc, carry):
        ls = pl.ds(pl.multiple_of(lc * LANES, LANES), LANES)
        for rc in range(RB // CONV_RC):
            acc = jnp.zeros((CONV_RC, LANES), F32)
            for s in range(SUBLANES):
                base = rc * CONV_RC + HALO - CONV_BACK - s
                us = ubuf[base:base + CONV_RC + CONV_BACK, ls]
                for m in range(CONV_BACK // SUBLANES + 1):
                    d = SUBLANES * m + s
                    if d >= CONV_W:
                        continue
                    lo = CONV_BACK - SUBLANES * m
                    acc = acc + cw_ref[CONV_W - 1 - d:CONV_W - d, ls] * us[lo:lo + CONV_RC, :]
            ybuf[rc * CONV_RC:(rc + 1) * CONV_RC, ls] = acc + cb_ref[:, ls]
        return carry

    lax.fori_loop(0, D // LANES, lane_chunk, 0)

    @pl.when(s == 0)
    def _():
        ubuf[0:HALO, :] = ubuf[N_META:N_META + HALO, :]

    @pl.when(s > 0)
    def _():
        ubuf[0:HALO, :] = ubuf[RB:RB + HALO, :]

    y = _layer_norm(ybuf[...], ng_ref[...], nb_ref[...])
    y_ref[...] = (y * _sigmoid(y)).astype(BF16)


def _conv_rowblk(b, s):
    return jnp.where(s == 0, META_BLK, b * BLK_PER_SEQ + s - 1)


def conv_module(p2, cw, cb, ng, nb):
    cwp = jnp.concatenate([cw, jnp.zeros((1, D), F32)], axis=0)
    return pl.pallas_call(
        _conv_kernel,
        grid=(BATCH, 1 + BLK_PER_SEQ),
        in_specs=[
            pl.BlockSpec((RB, D), lambda b, s: (_conv_rowblk(b, s), 0)),
            pl.BlockSpec((RB, D), lambda b, s: (_conv_rowblk(b, s), 1)),
            pl.BlockSpec((CONV_W + 1, D), lambda b, s: (0, 0)),
            pl.BlockSpec((1, D), lambda b, s: (0, 0)),
            pl.BlockSpec((1, D), lambda b, s: (0, 0)),
            pl.BlockSpec((1, D), lambda b, s: (0, 0)),
        ],
        out_specs=pl.BlockSpec((RB, D), lambda b, s: (_out_rowblk(b, s), 0)),
        out_shape=jax.ShapeDtypeStruct((R + RB, D), BF16),
        scratch_shapes=[pltpu.VMEM((HALO + RB, D), F32), pltpu.VMEM((RB, D), F32)],
        compiler_params=_cparams(("arbitrary", "arbitrary")),
        name="conv",
    )(p2, p2, cwp, cb.reshape(1, D), ng.reshape(1, D), nb.reshape(1, D))


def _merge_kernel(oa_ref, yc_ref, ga_ref, gb_ref, wg_ref, wc_ref, m_ref):
    ya = jnp.dot(oa_ref[...], wg_ref[...], preferred_element_type=F32)
    yb = jnp.dot(yc_ref[...], wc_ref[...], preferred_element_type=F32)
    m = _sigmoid(ga_ref[...].astype(F32)) * ya + _sigmoid(gb_ref[...].astype(F32)) * yb
    m_ref[...] = m.astype(BF16)


def merge(oa, yc, p2, wg, wc):
    wspec = pl.BlockSpec((D, D), lambda i: (0, 0), pipeline_mode=pl.Buffered(1))
    return pl.pallas_call(
        _merge_kernel,
        grid=(NRB,),
        in_specs=[pl.BlockSpec((RB, D), lambda i: (i, 0)),
                  pl.BlockSpec((RB, D), lambda i: (i, 0)),
                  pl.BlockSpec((RB, D), lambda i: (i, 2)),
                  pl.BlockSpec((RB, D), lambda i: (i, 3)),
                  wspec, wspec],
        out_specs=pl.BlockSpec((RB, D), lambda i: (i, 0)),
        out_shape=jax.ShapeDtypeStruct((R, D), BF16),
        compiler_params=_cparams(("arbitrary",)),
        name="merge",
    )(oa, yc, p2, p2, wg, wc)


def _out_ln_kernel(m_ref, h_ref, wo_ref, g_ref, b_ref, ho_ref, hb_ref):
    mix = jnp.dot(m_ref[...], wo_ref[...], preferred_element_type=F32)
    h = _layer_norm(ALPHA * h_ref[...] + mix, g_ref[...], b_ref[...])
    ho_ref[...] = h
    if hb_ref.dtype == jnp.uint32:
        hb_ref[...] = _pack_halves(h)
    else:
        hb_ref[...] = h.astype(BF16)


def out_ln(m, h, wo, g, b, *, packed):
    low = (jax.ShapeDtypeStruct((R, D // 2), jnp.uint32) if packed
           else jax.ShapeDtypeStruct((R, D), BF16))
    return pl.pallas_call(
        _out_ln_kernel,
        grid=(NRB,),
        in_specs=[pl.BlockSpec((RB, D), lambda i: (i, 0)),
                  pl.BlockSpec((RB, D), lambda i: (i, 0)),
                  pl.BlockSpec((D, D), lambda i: (0, 0), pipeline_mode=pl.Buffered(1)),
                  pl.BlockSpec((1, D), lambda i: (0, 0)),
                  pl.BlockSpec((1, D), lambda i: (0, 0))],
        out_specs=[pl.BlockSpec((RB, D), lambda i: (i, 0)),
                   pl.BlockSpec((RB, low.shape[1]), lambda i: (i, 0))],
        out_shape=[jax.ShapeDtypeStruct((R, D), F32), low],
        compiler_params=_cparams(("arbitrary",)),
        name="out_ln",
    )(m, h, wo, g.reshape(1, D), b.reshape(1, D))


def _router_kernel(h_ref, rw_ref, rb_ref, route_ref, cnt_ref, carry_ref):
    i = pl.program_id(0)

    @pl.when(i == 0)
    def _():
        carry_ref[...] = jnp.zeros_like(carry_ref)

    logits = jnp.dot(h_ref[...], rw_ref[...], preferred_element_type=F32,
                     precision=lax.Precision.HIGHEST) + rb_ref[...]
    lane = lax.broadcasted_iota(I32, (RB, LANES), 1).astype(F32)
    neg = jnp.float32(-jnp.inf)
    lg = jnp.where(lane < NE, logits, neg)
    m1 = jnp.max(lg, axis=1, keepdims=True)
    i1 = jnp.min(jnp.where(lg == m1, lane, float(LANES)), axis=1, keepdims=True)
    lg2 = jnp.where(lane == i1, neg, lg)
    m2 = jnp.max(lg2, axis=1, keepdims=True)
    i2 = jnp.min(jnp.where(lg2 == m2, lane, float(LANES)), axis=1, keepdims=True)
    e = jnp.exp(m2 - m1)
    g1 = 1.0 / (1.0 + e)
    g2 = e / (1.0 + e)

    rowid = i * RB + lax.broadcasted_iota(I32, (RB, 1), 0)
    is_tok = rowid < N_TOK
    hit1 = lane == i1
    hit2 = lane == i2
    onehot = jnp.where(jnp.logical_and(jnp.logical_or(hit1, hit2), is_tok), 1.0, 0.0)
    r_i = lax.broadcasted_iota(I32, (RB, RB), 0)
    c_i = lax.broadcasted_iota(I32, (RB, RB), 1)
    strict = jnp.where(r_i > c_i, 1.0, 0.0).astype(BF16)
    before = jnp.dot(strict, onehot.astype(BF16), preferred_element_type=F32) + carry_ref[0:1, :]
    pos1 = jnp.sum(jnp.where(hit1, before, 0.0), axis=1, keepdims=True)
    pos2 = jnp.sum(jnp.where(hit2, before, 0.0), axis=1, keepdims=True)
    carry_ref[...] = carry_ref[...] + jnp.sum(onehot, axis=0, keepdims=True)

    out = jnp.where(lane == 0, i1, 0.0)
    out = jnp.where(lane == 1, i2, out)
    out = jnp.where(lane == 2, pos1, out)
    out = jnp.where(lane == 3, pos2, out)
    out = jnp.where(lane == 4, g1, out)
    out = jnp.where(lane == 5, g2, out)
    route_ref[...] = out
    cnt_ref[...] = carry_ref[...]


def router(h, rw, rb):
    rwp = jnp.zeros((D, LANES), F32).at[:, :NE].set(rw)
    rbp = jnp.zeros((1, LANES), F32).at[0, :NE].set(rb)
    return pl.pallas_call(
        _router_kernel,
        grid=(NRB,),
        in_specs=[pl.BlockSpec((RB, D), lambda i: (i, 0)),
                  pl.BlockSpec((D, LANES), lambda i: (0, 0)),
                  pl.BlockSpec((1, LANES), lambda i: (0, 0))],
        out_specs=[pl.BlockSpec((RB, LANES), lambda i: (i, 0)),
                   pl.BlockSpec((8, LANES), lambda i: (0, 0))],
        out_shape=[jax.ShapeDtypeStruct((R, LANES), F32), jax.ShapeDtypeStruct((8, LANES), F32)],
        scratch_shapes=[pltpu.VMEM((8, LANES), F32)],
        compiler_params=_cparams(("arbitrary",)),
        name="router",
    )(h, rwp, rbp)


def _dispatch_kernel(d1_ref, d2_ref, hp_ref, xs_in_ref, xs_ref, sem):
    del xs_in_ref
    base = pl.program_id(0) * RB

    def copies(t):
        row = hp_ref.at[pl.ds(t, 1)]
        c1 = pltpu.make_async_copy(row, xs_ref.at[pl.ds(d1_ref[base + t], 1)], sem.at[0])
        c2 = pltpu.make_async_copy(row, xs_ref.at[pl.ds(d2_ref[base + t], 1)], sem.at[1])
        return c1, c2

    def start(t, c):
        c1, c2 = copies(t)
        c1.start()
        c2.start()
        return c

    def wait(t, c):
        c1, c2 = copies(t)
        c1.wait()
        c2.wait()
        return c

    def run(n):
        lax.fori_loop(0, n, start, 0, unroll=8)
        lax.fori_loop(0, n, wait, 0, unroll=8)

    @pl.when(pl.program_id(0) < META_BLK)
    def _():
        run(RB)

    @pl.when(pl.program_id(0) == META_BLK)
    def _():
        run(N_META)


def dispatch(hp, d1, d2):
    xs0 = jnp.zeros((N_SLOTS, D // 2), jnp.uint32)
    return pl.pallas_call(
        _dispatch_kernel,
        grid_spec=pltpu.PrefetchScalarGridSpec(
            num_scalar_prefetch=2,
            grid=(NRB,),
            in_specs=[pl.BlockSpec((RB, D // 2), lambda i, d1, d2: (i, 0)),
                      pl.BlockSpec(memory_space=pl.ANY)],
            out_specs=pl.BlockSpec(memory_space=pl.ANY),
            scratch_shapes=[pltpu.SemaphoreType.DMA((2,))]),
        out_shape=jax.ShapeDtypeStruct((N_SLOTS, D // 2), jnp.uint32),
        input_output_aliases={3: 0},
        compiler_params=_cparams(("arbitrary",)),
        name="dispatch",
    )(d1, d2, hp, xs0)


def _combine_kernel(d1_ref, d2_ref, y_ref, h_ref, route_ref, g_ref, b_ref, ho_ref, *rest):
    *maybe_hb, buf1, buf2, sem = rest
    i = pl.program_id(0)
    base = i * RB

    def copies(t):
        c1 = pltpu.make_async_copy(y_ref.at[pl.ds(d1_ref[base + t], 1)], buf1.at[pl.ds(t, 1)], sem.at[0])
        c2 = pltpu.make_async_copy(y_ref.at[pl.ds(d2_ref[base + t], 1)], buf2.at[pl.ds(t, 1)], sem.at[1])
        return c1, c2

    def start(t, c):
        c1, c2 = copies(t)
        c1.start()
        c2.start()
        return c

    def wait(t, c):
        c1, c2 = copies(t)
        c1.wait()
        c2.wait()
        return c

    lax.fori_loop(0, RB, start, 0, unroll=8)
    lax.fori_loop(0, RB, wait, 0, unroll=8)
    route = route_ref[...]
    g1 = route[:, 4:5]
    g2 = route[:, 5:6]
    f = g1 * buf1[...] + g2 * buf2[...]
    h = _layer_norm(ALPHA * h_ref[...] + f, g_ref[...], b_ref[...])
    ho_ref[...] = h
    for hb_ref in maybe_hb:
        hb_ref[...] = h.astype(BF16)


def combine(y, h, route, d1, d2, g, b, *, last):
    nblk = META_BLK if last else NRB
    row_spec = pl.BlockSpec((RB, D), lambda i, d1, d2: (i, 0))
    out_specs = [row_spec] if last else [row_spec, row_spec]
    out_shape = ([jax.ShapeDtypeStruct((R_MAIN, D), F32)] if last else
                 [jax.ShapeDtypeStruct((R, D), F32), jax.ShapeDtypeStruct((R, D), BF16)])
    return pl.pallas_call(
        _combine_kernel,
        grid_spec=pltpu.PrefetchScalarGridSpec(
            num_scalar_prefetch=2,
            grid=(nblk,),
            in_specs=[pl.BlockSpec(memory_space=pl.ANY),
                      row_spec,
                      pl.BlockSpec((RB, LANES), lambda i, d1, d2: (i, 0)),
                      pl.BlockSpec((1, D), lambda i, d1, d2: (0, 0)),
                      pl.BlockSpec((1, D), lambda i, d1, d2: (0, 0))],
            out_specs=out_specs,
            scratch_shapes=[pltpu.VMEM((RB, D), F32), pltpu.VMEM((RB, D), F32),
                            pltpu.SemaphoreType.DMA((2,))]),
        out_shape=out_shape,
        compiler_params=_cparams(("arbitrary",)),
        name="combine",
    )(d1, d2, y, h, route, g.reshape(1, D), b.reshape(1, D))


def _const_sched(nblk, e):
    return jnp.full((nblk,), e, I32), jnp.full((1,), nblk, I32)


def mixer_layer(i, h, hb, w_in, w_alpha_up, b_alpha, gla_norm_g, w_gla_o, conv_w, conv_b,
                conv_norm_g, conv_norm_b, w_conv_o, w_out, ln_g, ln_b, *, packed):
    be, nu = _const_sched(R // BIG, i)
    p1 = gmm(hb, w_in, be, nu, blk=BIG, tn=1024, n_out=COL_ALOW, name="in_qkvr")
    alow = gmm(hb, w_in, be, nu, blk=BIG, tn=128, n_out=128, col_off=COL_ALOW // 128,
               out_dtype=F32, name="in_alow")
    p2 = gmm_shift(hb, w_in, i, blk=BIG, tn=1024, n_out=4 * D, name="in_rest")
    wau = jnp.zeros((128, DK), BF16).at[:LOWRANK].set(w_alpha_up[i].astype(BF16))
    oa = gla(p1, alow, wau, b_alpha[i].reshape(1, DK), gla_norm_g[i].reshape(1, DV))
    yc = conv_module(p2, conv_w[i], conv_b[i], conv_norm_g[i], conv_norm_b[i])
    m = merge(oa, yc, p2, w_gla_o[i].astype(BF16), w_conv_o[i].astype(BF16))
    return out_ln(m, h, w_out[i].astype(BF16), ln_g[i], ln_b[i], packed=packed)


FFN_DOWN_BLK = 384


def dense_ffn_layer(j, h, hb, w1, w3, w2, ln_g, ln_b):
    be, nu = _const_sched(R // BIG, j)
    g = gateup(hb, w1, w3, be, nu, blk=BIG, tn=512, name="ffn_gateup")
    be2, nu2 = _const_sched(R // FFN_DOWN_BLK, j)
    y = gmm(g, w2, be2, nu2, blk=FFN_DOWN_BLK, tn=512, n_out=D, out_dtype=F32, name="ffn_down")
    return res_ln(h, y, ln_g, ln_b)


def moe_ffn_layer(j, h, hp, router_w, router_b, w1, w3, w2, ln_g, ln_b, *, last):
    route, cnt = router(h, router_w[j], router_b[j])
    counts = cnt[0, :NE].astype(I32)
    nblk_e = (counts + MOE_BLK - 1) // MOE_BLK
    blk_end = jnp.cumsum(nblk_e)
    blk_start = blk_end - nblk_e
    n_used = blk_end[NE - 1]
    blk_id = jnp.minimum(jnp.arange(MOE_NBLK, dtype=I32), n_used - 1)
    blk_exp = jnp.minimum(jnp.searchsorted(blk_end, blk_id, side="right"), NE - 1).astype(I32)
    e1 = route[:, 0].astype(I32)
    e2 = route[:, 1].astype(I32)
    is_tok = jnp.arange(R, dtype=I32) < N_TOK
    d1 = jnp.where(is_tok, blk_start[e1] * MOE_BLK + route[:, 2].astype(I32), 0)
    d2 = jnp.where(is_tok, blk_start[e2] * MOE_BLK + route[:, 3].astype(I32), 0)
    xs = dispatch(hp, d1, d2)
    be = blk_exp + j * NE
    nu = n_used.reshape(1)
    w1f = w1.reshape(-1, D, DFF)
    w3f = w3.reshape(-1, D, DFF)
    w2f = w2.reshape(-1, DFF, D)
    g = gateup(xs, w1f, w3f, be, nu, blk=MOE_BLK, tn=512, name="moe_gateup")
    y = gmm(g, w2f, be, nu, blk=MOE_BLK, tn=512, n_out=D, out_dtype=F32, name="moe_down")
    return combine(y, h, route, d1, d2, ln_g, ln_b, last=last)


def kernel(x, meta_tokens, ln_in_g, ln_in_b, w_in, w_alpha_up, b_alpha, gla_norm_g, w_gla_o, conv_w, conv_b, conv_norm_g, conv_norm_b, w_conv_o, w_out, ln_mix_g, ln_mix_b, ffn_w1, ffn_w3, ffn_w2, router_w, router_b, moe_w1, moe_w3, moe_w2, ln_ffn_g, ln_ffn_b):
    meta_blk = jnp.concatenate([meta_tokens, jnp.zeros((RB - N_META, D), F32)], axis=0)
    h, hb = ln_in(x.reshape(R_MAIN, D), meta_blk, ln_in_g, ln_in_b)
    for i in range(DEPTH):
        is_moe = i % 2 == 1
        h, hb = mixer_layer(i, h, hb, w_in, w_alpha_up, b_alpha, gla_norm_g, w_gla_o,
                            conv_w, conv_b, conv_norm_g, conv_norm_b, w_conv_o, w_out,
                            ln_mix_g, ln_mix_b, packed=is_moe)
        j = i // 2
        if not is_moe:
            h, hb = dense_ffn_layer(j, h, hb, ffn_w1, ffn_w3, ffn_w2, ln_ffn_g[i], ln_ffn_b[i])
        elif i < DEPTH - 1:
            h, hb = moe_ffn_layer(j, h, hb, router_w, router_b, moe_w1, moe_w3, moe_w2,
                                  ln_ffn_g[i], ln_ffn_b[i], last=False)
        else:
            (h,) = moe_ffn_layer(j, h, hb, router_w, router_b, moe_w1, moe_w3, moe_w2,
                                 ln_ffn_g[i], ln_ffn_b[i], last=True)
    return h.reshape(BATCH, SEQ, D)
```

```python
import functools

import jax
import jax.numpy as jnp
from jax import lax
from jax.experimental import pallas as pl
from jax.experimental.pallas import tpu as pltpu

F32 = jnp.float32
BF16 = jnp.bfloat16
I32 = jnp.int32

D = 2048
BATCH = 2
SEQ = 4096
DEPTH = 4
N_META = 16
HEADS = 4
DK = 1024
DV = 2048
DKH = DK // HEADS
DVH = DV // HEADS
LOWRANK = 16
TAU = 16.0
CHUNK = 64
CONV_W = 31
DFF = 5632
NE = 8
ALPHA = (2.0 * DEPTH) ** 0.25
EPS = 1e-5
D_IN = 14352
COL_ALOW = 6144
COL_REST = 6160

R_MAIN = BATCH * SEQ
META0 = R_MAIN
N_TOK = R_MAIN + N_META
RB = 256
R = 8448
NRB = R // RB
META_BLK = META0 // RB
BLK_PER_SEQ = SEQ // RB
BIG = 768

MOE_BLK = 512
N_ASSIGN = 2 * N_TOK
MOE_NBLK = -(-N_ASSIGN // MOE_BLK) + NE
N_SLOTS = MOE_NBLK * MOE_BLK

VMEM_LIMIT = 56 * 1024 * 1024


def _cparams(sem):
    return pltpu.CompilerParams(dimension_semantics=sem, vmem_limit_bytes=VMEM_LIMIT)


def _sigmoid(x):
    return 0.5 * jnp.tanh(0.5 * x) + 0.5


def _pack_halves(h):
    half = h.shape[1] // 2
    lo = lax.bitcast_convert_type(h[:, :half].astype(BF16).astype(F32), jnp.uint32)
    hi = lax.bitcast_convert_type(h[:, half:].astype(BF16).astype(F32), jnp.uint32)
    return (hi & jnp.uint32(0xFFFF0000)) | (lo >> jnp.uint32(16))


def _unpack_halves(p):
    lo = lax.bitcast_convert_type(p << jnp.uint32(16), F32).astype(BF16)
    hi = lax.bitcast_convert_type(p & jnp.uint32(0xFFFF0000), F32).astype(BF16)
    return jnp.concatenate([lo, hi], axis=1)


def _layer_norm(z, g, b):
    mu = jnp.mean(z, axis=-1, keepdims=True)
    zc = z - mu
    var = jnp.mean(zc * zc, axis=-1, keepdims=True)
    return zc * lax.rsqrt(var + EPS) * g + b


def _ln_in_kernel(x_ref, meta_ref, g_ref, b_ref, h_ref, hb_ref):
    def emit(rows):
        h = _layer_norm(rows, g_ref[...], b_ref[...])
        h_ref[...] = h
        hb_ref[...] = h.astype(BF16)

    @pl.when(pl.program_id(0) < META_BLK)
    def _():
        emit(x_ref[...])

    @pl.when(pl.program_id(0) == META_BLK)
    def _():
        emit(meta_ref[...])


def ln_in(x, meta_blk, g, b):
    return pl.pallas_call(
        _ln_in_kernel,
        grid=(NRB,),
        in_specs=[pl.BlockSpec((RB, D), lambda i: (jnp.minimum(i, META_BLK - 1), 0)),
                  pl.BlockSpec((RB, D), lambda i: (0, 0)),
                  pl.BlockSpec((1, D), lambda i: (0, 0)),
                  pl.BlockSpec((1, D), lambda i: (0, 0))],
        out_specs=[pl.BlockSpec((RB, D), lambda i: (i, 0)),
                   pl.BlockSpec((RB, D), lambda i: (i, 0))],
        out_shape=[jax.ShapeDtypeStruct((R, D), F32), jax.ShapeDtypeStruct((R, D), BF16)],
        compiler_params=_cparams(("arbitrary",)),
        name="ln_in",
    )(x, meta_blk, g.reshape(1, D), b.reshape(1, D))


def _res_ln_kernel(h_ref, f_ref, g_ref, b_ref, ho_ref, hb_ref):
    z = ALPHA * h_ref[...] + f_ref[...]
    h = _layer_norm(z, g_ref[...], b_ref[...])
    ho_ref[...] = h
    hb_ref[...] = h.astype(BF16)


def res_ln(h, f, g, b):
    return pl.pallas_call(
        _res_ln_kernel,
        grid=(NRB,),
        in_specs=[pl.BlockSpec((RB, D), lambda i: (i, 0)),
                  pl.BlockSpec((RB, D), lambda i: (i, 0)),
                  pl.BlockSpec((1, D), lambda i: (0, 0)),
                  pl.BlockSpec((1, D), lambda i: (0, 0))],
        out_specs=[pl.BlockSpec((RB, D), lambda i: (i, 0)),
                   pl.BlockSpec((RB, D), lambda i: (i, 0))],
        out_shape=[jax.ShapeDtypeStruct((R, D), F32), jax.ShapeDtypeStruct((R, D), BF16)],
        compiler_params=_cparams(("arbitrary",)),
        name="res_ln",
    )(h, f, g.reshape(1, D), b.reshape(1, D))


def _weight_changed(be_ref, b):
    prev = be_ref[jnp.maximum(b - 1, 0)]
    return jnp.logical_or(b == 0, be_ref[b] != prev)


def _gmm_kernel(be_ref, nu_ref, x_ref, w_ref, o_ref, wb_ref):
    b = pl.program_id(1)

    @pl.when(_weight_changed(be_ref, b))
    def _():
        wb_ref[...] = w_ref[...].astype(BF16)

    @pl.when(b < nu_ref[0])
    def _():
        x = x_ref[...].astype(BF16)
        o_ref[...] = jnp.dot(x, wb_ref[...], preferred_element_type=F32).astype(o_ref.dtype)

    @pl.when(b >= nu_ref[0])
    def _():
        o_ref[...] = jnp.zeros_like(o_ref)


def gmm(x, w, be, nu, *, blk, tn, n_out, col_off=0, out_dtype=BF16, name="gmm"):
    rows, k = x.shape
    grid = (n_out // tn, rows // blk)
    return pl.pallas_call(
        _gmm_kernel,
        grid_spec=pltpu.PrefetchScalarGridSpec(
            num_scalar_prefetch=2,
            grid=grid,
            in_specs=[pl.BlockSpec((blk, k), lambda j, b, be, nu: (jnp.minimum(b, nu[0] - 1), 0)),
                      pl.BlockSpec((None, k, tn), lambda j, b, be, nu: (be[b], 0, j + col_off))],
            out_specs=pl.BlockSpec((blk, tn), lambda j, b, be, nu: (b, j)),
            scratch_shapes=[pltpu.VMEM((k, tn), BF16)]),
        out_shape=jax.ShapeDtypeStruct((rows, n_out), out_dtype),
        compiler_params=_cparams(("arbitrary", "arbitrary")),
        name=name,
    )(be, nu, x, w)


SHIFT = COL_REST - COL_ALOW
NT_DIMS = (((1,), (1,)), ((), ()))


def _inproj_kernel(x_ref, wt_ref, *rest):
    *maybe_next, o_ref, wb_ref = rest

    @pl.when(pl.program_id(1) == 0)
    def _():
        if maybe_next:
            (wn_ref,) = maybe_next
            tn = wb_ref.shape[0]
            wb_ref[0:tn - SHIFT, :] = wt_ref[SHIFT:, :].astype(BF16)
            wb_ref[tn - SHIFT:, :] = wn_ref[0:SHIFT, :].astype(BF16)
        else:
            wb_ref[...] = wt_ref[...].astype(BF16)

    o_ref[...] = lax.dot_general(x_ref[...], wb_ref[...], NT_DIMS,
                                 preferred_element_type=F32).astype(o_ref.dtype)


def inproj(x, wt, layer, *, blk, tn, n_out, row0, shifted=False, out_dtype=BF16, name="inproj"):
    rows, k = x.shape
    grid = (n_out // tn, rows // blk)
    r0 = row0 // tn
    in_specs = [pl.BlockSpec((blk, k), lambda j, b: (b, 0)),
                pl.BlockSpec((None, tn, k), lambda j, b: (layer, r0 + j, 0))]
    args = [x, wt]
    if shifted:
        in_specs.append(pl.BlockSpec((None, LANES, k),
                                     lambda j, b: (layer, (r0 + j + 1) * (tn // LANES), 0)))
        args.append(wt)
    return pl.pallas_call(
        _inproj_kernel,
        grid=grid,
        in_specs=in_specs,
        out_specs=pl.BlockSpec((blk, tn), lambda j, b: (b, j)),
        out_shape=jax.ShapeDtypeStruct((rows, n_out), out_dtype),
        scratch_shapes=[pltpu.VMEM((tn, k), BF16)],
        compiler_params=_cparams(("arbitrary", "arbitrary")),
        name=name,
    )(*args)


def _gateup_kernel(be_ref, nu_ref, x_ref, w1_ref, w3_ref, o_ref, w1b_ref, w3b_ref):
    b = pl.program_id(1)

    @pl.when(_weight_changed(be_ref, b))
    def _():
        w1b_ref[...] = w1_ref[...].astype(BF16)
        w3b_ref[...] = w3_ref[...].astype(BF16)

    @pl.when(b < nu_ref[0])
    def _():
        if x_ref.dtype == jnp.uint32:
            x = _unpack_halves(x_ref[...])
        else:
            x = x_ref[...]
        a = jnp.dot(x, w1b_ref[...], preferred_element_type=F32)
        c = jnp.dot(x, w3b_ref[...], preferred_element_type=F32)
        o_ref[...] = ((a * _sigmoid(a)) * c).astype(o_ref.dtype)

    @pl.when(b >= nu_ref[0])
    def _():
        o_ref[...] = jnp.zeros_like(o_ref)


def gateup(x, w1, w3, be, nu, *, blk, tn, name="gateup"):
    rows, kx = x.shape
    k, n_out = w1.shape[-2:]
    grid = (n_out // tn, rows // blk)
    wspec = pl.BlockSpec((None, k, tn), lambda j, b, be, nu: (be[b], 0, j))
    return pl.pallas_call(
        _gateup_kernel,
        grid_spec=pltpu.PrefetchScalarGridSpec(
            num_scalar_prefetch=2,
            grid=grid,
            in_specs=[pl.BlockSpec((blk, kx), lambda j, b, be, nu: (jnp.minimum(b, nu[0] - 1), 0)),
                      wspec, wspec],
            out_specs=pl.BlockSpec((blk, tn), lambda j, b, be, nu: (b, j)),
            scratch_shapes=[pltpu.VMEM((k, tn), BF16), pltpu.VMEM((k, tn), BF16)]),
        out_shape=jax.ShapeDtypeStruct((rows, n_out), BF16),
        compiler_params=_cparams(("arbitrary", "arbitrary")),
        name=name,
    )(be, nu, x, w1, w3)


GLA_NCH = RB // CHUNK


def _gla_kernel(q_ref, k_ref, v_ref, r_ref, al_ref, wau_ref, ba_ref, gn_ref, o_ref, st_ref):
    s = pl.program_id(2)

    @pl.when(s == 0)
    def _():
        st_ref[...] = jnp.zeros_like(st_ref)

    row = lax.broadcasted_iota(I32, (CHUNK, CHUNK), 0)
    col = lax.broadcasted_iota(I32, (CHUNK, CHUNK), 1)
    causal = row >= col
    tril = jnp.where(causal, 1.0, 0.0).astype(BF16)
    rowid = lax.broadcasted_iota(I32, (CHUNK, 1), 0)
    wau = wau_ref[...]
    ba = ba_ref[...]
    gn = gn_ref[...]
    nt = (((1,), (1,)), ((), ()))
    tn = (((0,), (0,)), ((), ()))

    for ci in range(GLA_NCH):
        rs = pl.ds(ci * CHUNK, CHUNK)
        if ci == 0:
            valid = jnp.logical_or(s > 0, rowid < N_META)
        else:
            valid = jnp.logical_and(s > 0, rowid >= 0)
        x = jnp.dot(al_ref[rs, :].astype(BF16), wau, preferred_element_type=F32) + ba
        la = (jnp.minimum(x, 0.0) - jnp.log1p(jnp.exp(-jnp.abs(x)))) * (1.0 / TAU)
        la = jnp.where(valid, la, 0.0)
        la_hi = la.astype(BF16)
        la_lo = (la - la_hi.astype(F32)).astype(BF16)
        bcum = (jnp.dot(tril, la_hi, preferred_element_type=F32)
                + jnp.dot(tril, la_lo, preferred_element_type=F32))
        blast = bcum[CHUNK - 1:CHUNK, :]
        q = q_ref[rs, :].astype(F32) * (DKH ** -0.5)
        k = jnp.where(valid, k_ref[rs, :].astype(F32), 0.0)
        v = jnp.where(valid, v_ref[rs, :], jnp.zeros((), BF16))
        qd = (q * jnp.exp(bcum)).astype(BF16)
        kd = (k * jnp.exp(-bcum)).astype(BF16)
        ke = (k * jnp.exp(blast - bcum)).astype(BF16)
        dec = jnp.exp(blast)
        sc = lax.dot_general(qd, kd, nt, preferred_element_type=F32)
        sc = jnp.where(causal, sc, 0.0).astype(BF16)
        st = st_ref[...]
        o = (jnp.dot(sc, v, preferred_element_type=F32)
             + lax.dot_general(qd, st.astype(BF16), nt, preferred_element_type=F32))
        st_ref[...] = dec * st + lax.dot_general(v, ke, tn, preferred_element_type=F32)
        ms = jnp.mean(o * o, axis=-1, keepdims=True)
        on = o * lax.rsqrt(ms + EPS) * gn
        r = r_ref[rs, :].astype(F32)
        o_ref[rs, :] = ((r * _sigmoid(r)) * on).astype(BF16)


def _gla_rowblk(b, s):
    return jnp.where(s == 0, META_BLK, b * BLK_PER_SEQ + s - 1)


def _out_rowblk(b, s):
    return jnp.where(s == 0, META_BLK + jnp.minimum(b, 1), b * BLK_PER_SEQ + s - 1)


def gla(p1, alow, wau, ba, gn):
    nq = DK // DKH
    return pl.pallas_call(
        _gla_kernel,
        grid=(BATCH, HEADS, 1 + BLK_PER_SEQ),
        in_specs=[
            pl.BlockSpec((RB, DKH), lambda b, h, s: (_gla_rowblk(b, s), h)),
            pl.BlockSpec((RB, DKH), lambda b, h, s: (_gla_rowblk(b, s), nq + h)),
            pl.BlockSpec((RB, DVH), lambda b, h, s: (_gla_rowblk(b, s), 2 * DK // DVH + h)),
            pl.BlockSpec((RB, DVH), lambda b, h, s: (_gla_rowblk(b, s), (2 * DK + DV) // DVH + h)),
            pl.BlockSpec((RB, 128), lambda b, h, s: (_gla_rowblk(b, s), 0)),
            pl.BlockSpec((128, DKH), lambda b, h, s: (0, h)),
            pl.BlockSpec((1, DKH), lambda b, h, s: (0, h)),
            pl.BlockSpec((1, DVH), lambda b, h, s: (0, h)),
        ],
        out_specs=pl.BlockSpec((RB, DVH), lambda b, h, s: (_out_rowblk(b, s), h)),
        out_shape=jax.ShapeDtypeStruct((R + RB, DV), BF16),
        scratch_shapes=[pltpu.VMEM((DVH, DKH), F32)],
        compiler_params=_cparams(("arbitrary", "arbitrary", "arbitrary")),
        name="gla",
    )(p1, p1, p1, p1, alow, wau, ba, gn)


HALO = 32
CONV_RC = 128
LANES = 128
SUBLANES = 8
CONV_BACK = 24
CONV_ZR = CONV_RC + SUBLANES


def _conv_kernel(a_ref, g_ref, cw_ref, cb_ref, ng_ref, nb_ref, y_ref, ubuf, ybuf, zbuf):
    s = pl.program_id(1)
    u = a_ref[...].astype(F32) * _sigmoid(g_ref[...].astype(F32))

    @pl.when(s == 0)
    def _():
        rowid = lax.broadcasted_iota(I32, (RB, 1), 0)
        ubuf[0:HALO, :] = jnp.zeros((HALO, D), F32)
        ubuf[HALO:HALO + RB, :] = jnp.where(rowid < N_META, u, 0.0)

    @pl.when(s > 0)
    def _():
        ubuf[HALO:HALO + RB, :] = u

    def lane_chunk(lc, carry):
        ls = pl.ds(pl.multiple_of(lc * LANES, LANES), LANES)
        for rc in range(RB // CONV_RC):
            r0 = rc * CONV_RC + HALO
            for s in range(SUBLANES):
                z = None
                for m in range(CONV_BACK // SUBLANES + 1):
                    d = SUBLANES * m + s
                    if d >= CONV_W:
                        continue
                    lo = r0 - SUBLANES - SUBLANES * m
                    term = cw_ref[CONV_W - 1 - d:CONV_W - d, ls] * ubuf[lo:lo + CONV_ZR, ls]
                    z = term if z is None else z + term
                zbuf[s * CONV_ZR:(s + 1) * CONV_ZR, :] = z
            acc = zbuf[SUBLANES:SUBLANES + CONV_RC, :] + cb_ref[:, ls]
            for s in range(1, SUBLANES):
                lo = s * CONV_ZR + SUBLANES - s
                acc = acc + zbuf[lo:lo + CONV_RC, :]
            ybuf[rc * CONV_RC:(rc + 1) * CONV_RC, ls] = acc
        return carry

    lax.fori_loop(0, D // LANES, lane_chunk, 0)

    @pl.when(s == 0)
    def _():
        ubuf[0:HALO, :] = ubuf[N_META:N_META + HALO, :]

    @pl.when(s > 0)
    def _():
        ubuf[0:HALO, :] = ubuf[RB:RB + HALO, :]

    y = _layer_norm(ybuf[...], ng_ref[...], nb_ref[...])
    y_ref[...] = (y * _sigmoid(y)).astype(BF16)


def _conv_rowblk(b, s):
    return jnp.where(s == 0, META_BLK, b * BLK_PER_SEQ + s - 1)


def conv_module(p2, cw, cb, ng, nb):
    cwp = jnp.concatenate([cw, jnp.zeros((1, D), F32)], axis=0)
    return pl.pallas_call(
        _conv_kernel,
        grid=(BATCH, 1 + BLK_PER_SEQ),
        in_specs=[
            pl.BlockSpec((RB, D), lambda b, s: (_conv_rowblk(b, s), 0)),
            pl.BlockSpec((RB, D), lambda b, s: (_conv_rowblk(b, s), 1)),
            pl.BlockSpec((CONV_W + 1, D), lambda b, s: (0, 0)),
            pl.BlockSpec((1, D), lambda b, s: (0, 0)),
            pl.BlockSpec((1, D), lambda b, s: (0, 0)),
            pl.BlockSpec((1, D), lambda b, s: (0, 0)),
        ],
        out_specs=pl.BlockSpec((RB, D), lambda b, s: (_out_rowblk(b, s), 0)),
        out_shape=jax.ShapeDtypeStruct((R + RB, D), BF16),
        scratch_shapes=[pltpu.VMEM((HALO + RB, D), F32), pltpu.VMEM((RB, D), F32),
                        pltpu.VMEM((SUBLANES * CONV_ZR, LANES), F32)],
        compiler_params=_cparams(("arbitrary", "arbitrary")),
        name="conv",
    )(p2, p2, cwp, cb.reshape(1, D), ng.reshape(1, D), nb.reshape(1, D))


def _merge_kernel(oa_ref, yc_ref, ga_ref, gb_ref, wg_ref, wc_ref, m_ref):
    ya = jnp.dot(oa_ref[...], wg_ref[...], preferred_element_type=F32)
    yb = jnp.dot(yc_ref[...], wc_ref[...], preferred_element_type=F32)
    m = _sigmoid(ga_ref[...].astype(F32)) * ya + _sigmoid(gb_ref[...].astype(F32)) * yb
    m_ref[...] = m.astype(BF16)


def merge(oa, yc, p2, wg, wc, layer):
    wspec = pl.BlockSpec((None, D, D), lambda i: (layer, 0, 0), pipeline_mode=pl.Buffered(1))
    return pl.pallas_call(
        _merge_kernel,
        grid=(NRB,),
        in_specs=[pl.BlockSpec((RB, D), lambda i: (i, 0)),
                  pl.BlockSpec((RB, D), lambda i: (i, 0)),
                  pl.BlockSpec((RB, D), lambda i: (i, 2)),
                  pl.BlockSpec((RB, D), lambda i: (i, 3)),
                  wspec, wspec],
        out_specs=pl.BlockSpec((RB, D), lambda i: (i, 0)),
        out_shape=jax.ShapeDtypeStruct((R, D), BF16),
        compiler_params=_cparams(("arbitrary",)),
        name="merge",
    )(oa, yc, p2, p2, wg, wc)


def _out_ln_kernel(m_ref, h_ref, wo_ref, g_ref, b_ref, ho_ref, hb_ref):
    mix = jnp.dot(m_ref[...], wo_ref[...], preferred_element_type=F32)
    h = _layer_norm(ALPHA * h_ref[...] + mix, g_ref[...], b_ref[...])
    ho_ref[...] = h
    if hb_ref.dtype == jnp.uint32:
        hb_ref[...] = _pack_halves(h)
    else:
        hb_ref[...] = h.astype(BF16)


def out_ln(m, h, wo, layer, g, b, *, packed):
    low = (jax.ShapeDtypeStruct((R, D // 2), jnp.uint32) if packed
           else jax.ShapeDtypeStruct((R, D), BF16))
    return pl.pallas_call(
        _out_ln_kernel,
        grid=(NRB,),
        in_specs=[pl.BlockSpec((RB, D), lambda i: (i, 0)),
                  pl.BlockSpec((RB, D), lambda i: (i, 0)),
                  pl.BlockSpec((None, D, D), lambda i: (layer, 0, 0), pipeline_mode=pl.Buffered(1)),
                  pl.BlockSpec((1, D), lambda i: (0, 0)),
                  pl.BlockSpec((1, D), lambda i: (0, 0))],
        out_specs=[pl.BlockSpec((RB, D), lambda i: (i, 0)),
                   pl.BlockSpec((RB, low.shape[1]), lambda i: (i, 0))],
        out_shape=[jax.ShapeDtypeStruct((R, D), F32), low],
        compiler_params=_cparams(("arbitrary",)),
        name="out_ln",
    )(m, h, wo, g.reshape(1, D), b.reshape(1, D))


def _router_kernel(h_ref, rw_ref, rb_ref, route_ref, cnt_ref, carry_ref):
    i = pl.program_id(0)

    @pl.when(i == 0)
    def _():
        carry_ref[...] = jnp.zeros_like(carry_ref)

    logits = jnp.dot(h_ref[...], rw_ref[...], preferred_element_type=F32,
                     precision=lax.Precision.HIGHEST) + rb_ref[...]
    lane = lax.broadcasted_iota(I32, (RB, LANES), 1).astype(F32)
    neg = jnp.float32(-jnp.inf)
    lg = jnp.where(lane < NE, logits, neg)
    m1 = jnp.max(lg, axis=1, keepdims=True)
    i1 = jnp.min(jnp.where(lg == m1, lane, float(LANES)), axis=1, keepdims=True)
    lg2 = jnp.where(lane == i1, neg, lg)
    m2 = jnp.max(lg2, axis=1, keepdims=True)
    i2 = jnp.min(jnp.where(lg2 == m2, lane, float(LANES)), axis=1, keepdims=True)
    e = jnp.exp(m2 - m1)
    g1 = 1.0 / (1.0 + e)
    g2 = e / (1.0 + e)

    rowid = i * RB + lax.broadcasted_iota(I32, (RB, 1), 0)
    is_tok = rowid < N_TOK
    hit1 = lane == i1
    hit2 = lane == i2
    onehot = jnp.where(jnp.logical_and(jnp.logical_or(hit1, hit2), is_tok), 1.0, 0.0)
    r_i = lax.broadcasted_iota(I32, (RB, RB), 0)
    c_i = lax.broadcasted_iota(I32, (RB, RB), 1)
    strict = jnp.where(r_i > c_i, 1.0, 0.0).astype(BF16)
    before = jnp.dot(strict, onehot.astype(BF16), preferred_element_type=F32) + carry_ref[0:1, :]
    pos1 = jnp.sum(jnp.where(hit1, before, 0.0), axis=1, keepdims=True)
    pos2 = jnp.sum(jnp.where(hit2, before, 0.0), axis=1, keepdims=True)
    carry_ref[...] = carry_ref[...] + jnp.sum(onehot, axis=0, keepdims=True)

    out = jnp.where(lane == 0, i1, 0.0)
    out = jnp.where(lane == 1, i2, out)
    out = jnp.where(lane == 2, pos1, out)
    out = jnp.where(lane == 3, pos2, out)
    out = jnp.where(lane == 4, g1, out)
    out = jnp.where(lane == 5, g2, out)
    route_ref[...] = out
    cnt_ref[...] = carry_ref[...]


def router(h, rw, rb):
    rwp = jnp.zeros((D, LANES), F32).at[:, :NE].set(rw)
    rbp = jnp.zeros((1, LANES), F32).at[0, :NE].set(rb)
    return pl.pallas_call(
        _router_kernel,
        grid=(NRB,),
        in_specs=[pl.BlockSpec((RB, D), lambda i: (i, 0)),
                  pl.BlockSpec((D, LANES), lambda i: (0, 0)),
                  pl.BlockSpec((1, LANES), lambda i: (0, 0))],
        out_specs=[pl.BlockSpec((RB, LANES), lambda i: (i, 0)),
                   pl.BlockSpec((8, LANES), lambda i: (0, 0))],
        out_shape=[jax.ShapeDtypeStruct((R, LANES), F32), jax.ShapeDtypeStruct((8, LANES), F32)],
        scratch_shapes=[pltpu.VMEM((8, LANES), F32)],
        compiler_params=_cparams(("arbitrary",)),
        name="router",
    )(h, rwp, rbp)


def _dispatch_kernel(d1_ref, d2_ref, hp_ref, xs_in_ref, xs_ref, sem):
    del xs_in_ref
    base = pl.program_id(0) * RB

    def copies(t):
        row = hp_ref.at[pl.ds(t, 1)]
        c1 = pltpu.make_async_copy(row, xs_ref.at[pl.ds(d1_ref[base + t], 1)], sem.at[0])
        c2 = pltpu.make_async_copy(row, xs_ref.at[pl.ds(d2_ref[base + t], 1)], sem.at[1])
        return c1, c2

    def start(t, c):
        c1, c2 = copies(t)
        c1.start()
        c2.start()
        return c

    def wait(t, c):
        c1, c2 = copies(t)
        c1.wait()
        c2.wait()
        return c

    def run(n):
        lax.fori_loop(0, n, start, 0, unroll=8)
        lax.fori_loop(0, n, wait, 0, unroll=8)

    @pl.when(pl.program_id(0) < META_BLK)
    def _():
        run(RB)

    @pl.when(pl.program_id(0) == META_BLK)
    def _():
        run(N_META)


def dispatch(hp, d1, d2):
    xs0 = jnp.zeros((N_SLOTS, D // 2), jnp.uint32)
    return pl.pallas_call(
        _dispatch_kernel,
        grid_spec=pltpu.PrefetchScalarGridSpec(
            num_scalar_prefetch=2,
            grid=(NRB,),
            in_specs=[pl.BlockSpec((RB, D // 2), lambda i, d1, d2: (i, 0)),
                      pl.BlockSpec(memory_space=pl.ANY)],
            out_specs=pl.BlockSpec(memory_space=pl.ANY),
            scratch_shapes=[pltpu.SemaphoreType.DMA((2,))]),
        out_shape=jax.ShapeDtypeStruct((N_SLOTS, D // 2), jnp.uint32),
        input_output_aliases={3: 0},
        compiler_params=_cparams(("arbitrary",)),
        name="dispatch",
    )(d1, d2, hp, xs0)


def _combine_kernel(d1_ref, d2_ref, y_ref, h_ref, route_ref, g_ref, b_ref, ho_ref, *rest):
    *maybe_hb, buf1, buf2, sem = rest
    i = pl.program_id(0)
    base = i * RB

    def copies(t):
        c1 = pltpu.make_async_copy(y_ref.at[pl.ds(d1_ref[base + t], 1)], buf1.at[pl.ds(t, 1)], sem.at[0])
        c2 = pltpu.make_async_copy(y_ref.at[pl.ds(d2_ref[base + t], 1)], buf2.at[pl.ds(t, 1)], sem.at[1])
        return c1, c2

    def start(t, c):
        c1, c2 = copies(t)
        c1.start()
        c2.start()
        return c

    def wait(t, c):
        c1, c2 = copies(t)
        c1.wait()
        c2.wait()
        return c

    lax.fori_loop(0, RB, start, 0, unroll=8)
    lax.fori_loop(0, RB, wait, 0, unroll=8)
    route = route_ref[...]
    g1 = route[:, 4:5]
    g2 = route[:, 5:6]
    f = g1 * buf1[...] + g2 * buf2[...]
    h = _layer_norm(ALPHA * h_ref[...] + f, g_ref[...], b_ref[...])
    ho_ref[...] = h
    for hb_ref in maybe_hb:
        hb_ref[...] = h.astype(BF16)


def combine(y, h, route, d1, d2, g, b, *, last):
    nblk = META_BLK if last else NRB
    row_spec = pl.BlockSpec((RB, D), lambda i, d1, d2: (i, 0))
    out_specs = [row_spec] if last else [row_spec, row_spec]
    out_shape = ([jax.ShapeDtypeStruct((R_MAIN, D), F32)] if last else
                 [jax.ShapeDtypeStruct((R, D), F32), jax.ShapeDtypeStruct((R, D), BF16)])
    return pl.pallas_call(
        _combine_kernel,
        grid_spec=pltpu.PrefetchScalarGridSpec(
            num_scalar_prefetch=2,
            grid=(nblk,),
            in_specs=[pl.BlockSpec(memory_space=pl.ANY),
                      row_spec,
                      pl.BlockSpec((RB, LANES), lambda i, d1, d2: (i, 0)),
                      pl.BlockSpec((1, D), lambda i, d1, d2: (0, 0)),
                      pl.BlockSpec((1, D), lambda i, d1, d2: (0, 0))],
            out_specs=out_specs,
            scratch_shapes=[pltpu.VMEM((RB, D), F32), pltpu.VMEM((RB, D), F32),
                            pltpu.SemaphoreType.DMA((2,))]),
        out_shape=out_shape,
        compiler_params=_cparams(("arbitrary",)),
        name="combine",
    )(d1, d2, y, h, route, g.reshape(1, D), b.reshape(1, D))


def _const_sched(nblk, e):
    return jnp.full((nblk,), e, I32), jnp.full((1,), nblk, I32)


def mixer_layer(i, h, hb, w_in_t, w_alpha_up, b_alpha, gla_norm_g, w_gla_o, conv_w, conv_b,
                conv_norm_g, conv_norm_b, w_conv_o, w_out, ln_g, ln_b, *, packed):
    p1 = inproj(hb, w_in_t, i, blk=BIG, tn=1024, n_out=COL_ALOW, row0=0, name="in_qkvr")
    alow = inproj(hb, w_in_t, i, blk=BIG, tn=128, n_out=128, row0=COL_ALOW, out_dtype=F32,
                  name="in_alow")
    p2 = inproj(hb, w_in_t, i, blk=BIG, tn=1024, n_out=4 * D, row0=COL_ALOW, shifted=True,
                name="in_rest")
    wau = jnp.zeros((128, DK), BF16).at[:LOWRANK].set(w_alpha_up[i].astype(BF16))
    oa = gla(p1, alow, wau, b_alpha[i].reshape(1, DK), gla_norm_g[i].reshape(1, DV))
    yc = conv_module(p2, conv_w[i], conv_b[i], conv_norm_g[i], conv_norm_b[i])
    m = merge(oa, yc, p2, w_gla_o, w_conv_o, i)
    return out_ln(m, h, w_out, i, ln_g[i], ln_b[i], packed=packed)


FFN_DOWN_BLK = 384


def dense_ffn_layer(j, h, hb, w1, w3, w2, ln_g, ln_b):
    be, nu = _const_sched(R // BIG, j)
    g = gateup(hb, w1, w3, be, nu, blk=BIG, tn=512, name="ffn_gateup")
    be2, nu2 = _const_sched(R // FFN_DOWN_BLK, j)
    y = gmm(g, w2, be2, nu2, blk=FFN_DOWN_BLK, tn=512, n_out=D, out_dtype=F32, name="ffn_down")
    return res_ln(h, y, ln_g, ln_b)


def moe_ffn_layer(j, h, hp, router_w, router_b, w1, w3, w2, ln_g, ln_b, *, last):
    route, cnt = router(h, router_w[j], router_b[j])
    counts = cnt[0, :NE].astype(I32)
    nblk_e = (counts + MOE_BLK - 1) // MOE_BLK
    blk_end = jnp.cumsum(nblk_e)
    blk_start = blk_end - nblk_e
    n_used = blk_end[NE - 1]
    blk_id = jnp.minimum(jnp.arange(MOE_NBLK, dtype=I32), n_used - 1)
    blk_exp = jnp.minimum(jnp.searchsorted(blk_end, blk_id, side="right"), NE - 1).astype(I32)
    e1 = route[:, 0].astype(I32)
    e2 = route[:, 1].astype(I32)
    is_tok = jnp.arange(R, dtype=I32) < N_TOK
    d1 = jnp.where(is_tok, blk_start[e1] * MOE_BLK + route[:, 2].astype(I32), 0)
    d2 = jnp.where(is_tok, blk_start[e2] * MOE_BLK + route[:, 3].astype(I32), 0)
    xs = dispatch(hp, d1, d2)
    be = blk_exp + j * NE
    nu = n_used.reshape(1)
    w1f = w1.reshape(-1, D, DFF)
    w3f = w3.reshape(-1, D, DFF)
    w2f = w2.reshape(-1, DFF, D)
    g = gateup(xs, w1f, w3f, be, nu, blk=MOE_BLK, tn=512, name="moe_gateup")
    y = gmm(g, w2f, be, nu, blk=MOE_BLK, tn=512, n_out=D, out_dtype=F32, name="moe_down")
    return combine(y, h, route, d1, d2, ln_g, ln_b, last=last)


def kernel(x, meta_tokens, ln_in_g, ln_in_b, w_in, w_alpha_up, b_alpha, gla_norm_g, w_gla_o, conv_w, conv_b, conv_norm_g, conv_norm_b, w_conv_o, w_out, ln_mix_g, ln_mix_b, ffn_w1, ffn_w3, ffn_w2, router_w, router_b, moe_w1, moe_w3, moe_w2, ln_ffn_g, ln_ffn_b):
    meta_blk = jnp.concatenate([meta_tokens, jnp.zeros((RB - N_META, D), F32)], axis=0)
    h, hb = ln_in(x.reshape(R_MAIN, D), meta_blk, ln_in_g, ln_in_b)
    w_in_t = jnp.swapaxes(w_in, 1, 2)
    w_gla_o, w_conv_o, w_out = (w.astype(BF16) for w in (w_gla_o, w_conv_o, w_out))
    for i in range(DEPTH):
        is_moe = i % 2 == 1
        h, hb = mixer_layer(i, h, hb, w_in_t, w_alpha_up, b_alpha, gla_norm_g, w_gla_o,
                            conv_w, conv_b, conv_norm_g, conv_norm_b, w_conv_o, w_out,
                            ln_mix_g, ln_mix_b, packed=is_moe)
        j = i // 2
        if not is_moe:
            h, hb = dense_ffn_layer(j, h, hb, ffn_w1, ffn_w3, ffn_w2, ln_ffn_g[i], ln_ffn_b[i])
        elif i < DEPTH - 1:
            h, hb = moe_ffn_layer(j, h, hb, router_w, router_b, moe_w1, moe_w3, moe_w2,
                                  ln_ffn_g[i], ln_ffn_b[i], last=False)
        else:
            (h,) = moe_ffn_layer(j, h, hb, router_w, router_b, moe_w1, moe_w3, moe_w2,
                                 ln_ffn_g[i], ln_ffn_b[i], last=True)
    return h.reshape(BATCH, SEQ, D)
```
